```python
import jax, jax.numpy as jnp
from jax import lax
import numpy as np

D_MODEL = 2048
BATCH = 4
SEQ = 4096
DEPTH = 1

HEAD_DIM = 128
DILATION_PAIRS = ((128, 1), (512, 4), (2048, 16))
N_GROUPS = len(DILATION_PAIRS)
HEADS_PER_GROUP = D_MODEL // (4 * HEAD_DIM)
N_ATTN_HEADS = N_GROUPS * HEADS_PER_GROUP
ATTN_WIDTH = N_ATTN_HEADS * HEAD_DIM
ATTN_OUT_WIDTH = HEADS_PER_GROUP * HEAD_DIM
ROT_DIM = HEAD_DIM // 4
ROPE_THETA = 500000.0
POOL_WINDOWS = (2, 4, 8, 16)
POOL_WIDTH = D_MODEL // 4
POOL_GROUP_WIDTH = POOL_WIDTH // len(POOL_WINDOWS)
IN_WIDTH = 3 * ATTN_WIDTH + POOL_WIDTH
D_FF = 3 * D_MODEL
CONV_WIDTH = 3
NORM_EPS = 1e-6
NEG_INF = -1e30

kernel_name = 'hybrid_dilated_attn_pool_convffn_encoder'


def rmsnorm(x, g):
    xf = x.astype(jnp.float32)
    y = xf * lax.rsqrt(jnp.mean(xf * xf, axis=-1, keepdims=True) + NORM_EPS)
    return (y * g.astype(jnp.float32)).astype(x.dtype)


def partial_rotary(x, pos):
    half = ROT_DIM // 2
    inv_freq = ROPE_THETA ** (-jnp.arange(0, ROT_DIM, 2, dtype=jnp.float32) / ROT_DIM)
    ang = pos.astype(jnp.float32)[:, None] * inv_freq[None, :]
    cos = jnp.cos(ang)[None, :, None, :]
    sin = jnp.sin(ang)[None, :, None, :]
    xf = x.astype(jnp.float32)
    x1, x2, rest = xf[..., :half], xf[..., half:ROT_DIM], xf[..., ROT_DIM:]
    out = jnp.concatenate([x1 * cos - x2 * sin, x2 * cos + x1 * sin, rest], axis=-1)
    return out.astype(x.dtype)


def banded_attention(q, k, v, radius):
    L, Dh = q.shape[-2], q.shape[-1]
    lead = q.shape[:-2]
    blk = radius
    nb = -(-L // blk)
    Lp = nb * blk
    npad = [(0, 0)] * len(lead)
    qb = jnp.pad(q, npad + [(0, Lp - L), (0, 0)]).reshape(*lead, nb, blk, Dh)
    kb = jnp.pad(k, npad + [(blk, Lp - L + blk), (0, 0)]).reshape(*lead, nb + 2, blk, Dh)
    vb = jnp.pad(v, npad + [(blk, Lp - L + blk), (0, 0)]).reshape(*lead, nb + 2, blk, Dh)

    def window(xb):
        return jnp.concatenate([xb[..., :-2, :, :], xb[..., 1:-1, :, :], xb[..., 2:, :, :]], axis=-2)

    kw, vw = window(kb), window(vb)
    s = jnp.einsum('...nqd,...nkd->...nqk', qb.astype(jnp.float32), kw.astype(jnp.float32)) * (Dh ** -0.5)
    qi = jnp.arange(nb)[:, None, None] * blk + jnp.arange(blk)[None, :, None]
    kj = jnp.arange(nb)[:, None, None] * blk - blk + jnp.arange(3 * blk)[None, None, :]
    valid = (jnp.abs(kj - qi) <= radius) & (kj >= 0) & (kj < L)
    s = jnp.where(valid, s, NEG_INF)
    m = jnp.max(s, axis=-1, keepdims=True)
    p = jnp.where(valid, jnp.exp(s - m), 0.0)
    den = jnp.sum(p, axis=-1, keepdims=True)
    o = jnp.einsum('...nqk,...nkd->...nqd', p, vw.astype(jnp.float32)) / den
    lse = (m + jnp.log(den))[..., 0]
    o = o.reshape(*lead, Lp, Dh)[..., :L, :]
    lse = lse.reshape(*lead, Lp)[..., :L]
    return o, lse


def dilated_attention(q, k, v, dilation, radius):
    B, H, S, Dh = q.shape
    L = S // dilation

    def split(t):
        return t.reshape(B, H, L, dilation, Dh).transpose(0, 1, 3, 2, 4)

    o, lse = banded_attention(split(q), split(k), split(v), radius)
    o = o.transpose(0, 1, 3, 2, 4).reshape(B, H, S, Dh)
    lse = lse.transpose(0, 1, 3, 2).reshape(B, H, S)
    return o, lse


def multiscale_pool(p):
    S = p.shape[1]
    pf = p.astype(jnp.float32)
    c = jnp.concatenate([jnp.zeros_like(pf[:, :1]), jnp.cumsum(pf, axis=1)], axis=1)
    t = jnp.arange(S)
    outs = []
    for g, w in enumerate(POOL_WINDOWS):
        sl = slice(g * POOL_GROUP_WIDTH, (g + 1) * POOL_GROUP_WIDTH)
        cg = c[..., sl]
        lo = jnp.maximum(t - w // 2, 0)
        hi = jnp.minimum(t + w // 2, S)
        cnt = (hi - lo).astype(jnp.float32)[None, :, None]
        mean = (jnp.take(cg, hi, axis=1) - jnp.take(cg, lo, axis=1)) / cnt
        outs.append(mean - pf[..., sl])
    return jnp.concatenate(outs, axis=-1).astype(p.dtype)


def depthwise_conv3(a, w, b):
    ap = jnp.pad(a, ((0, 0), (1, 1), (0, 0)))
    return ap[:, :-2] * w[0] + ap[:, 1:-1] * w[1] + ap[:, 2:] * w[2] + b


def setup_inputs(seed: int = 0) -> dict:
    key = jax.random.key(seed)
    ks = jax.random.split(key, 16)
    f32 = jnp.float32

    def dense(k, shape, fan_in):
        return jax.random.normal(k, shape, f32) * (fan_in ** -0.5)

    return {
        'x': jax.random.normal(ks[0], (BATCH, SEQ, D_MODEL), f32),
        'norm_mix_g': 1.0 + 0.02 * jax.random.normal(ks[1], (D_MODEL,), f32),
        'w_in': dense(ks[2], (D_MODEL, IN_WIDTH), D_MODEL),
        'w_attn_out': dense(ks[3], (ATTN_OUT_WIDTH, D_MODEL), ATTN_OUT_WIDTH),
        'pool_w': dense(ks[4], (len(POOL_WINDOWS), POOL_GROUP_WIDTH, POOL_GROUP_WIDTH), POOL_GROUP_WIDTH),
        'pool_scale': 1.0 + 0.02 * jax.random.normal(ks[5], (POOL_WIDTH,), f32),
        'w_pool_out': dense(ks[6], (POOL_WIDTH, D_MODEL), POOL_WIDTH),
        'w_gate': dense(ks[7], (D_MODEL, 2 * D_MODEL), D_MODEL),
        'w_out': dense(ks[8], (D_MODEL, D_MODEL), D_MODEL),
        'norm_ffn_g': 1.0 + 0.02 * jax.random.normal(ks[9], (D_MODEL,), f32),
        'w_up': dense(ks[10], (D_MODEL, 2 * D_FF), D_MODEL),
        'conv_w': dense(ks[11], (CONV_WIDTH, D_FF), CONV_WIDTH),
        'conv_b': 0.01 * jax.random.normal(ks[12], (D_FF,), f32),
        'w_down': dense(ks[13], (D_FF, D_MODEL), D_FF),
        'norm_final_g': 1.0 + 0.02 * jax.random.normal(ks[14], (D_MODEL,), f32),
    }


def reference(x, norm_mix_g, w_in, w_attn_out, pool_w, pool_scale, w_pool_out, w_gate, w_out,
              norm_ffn_g, w_up, conv_w, conv_b, w_down, norm_final_g):
    B, S, _ = x.shape
    pos = jnp.arange(S)
    h = x
    for _layer in range(DEPTH):
        u = rmsnorm(h, norm_mix_g)
        proj = u @ w_in
        q, k, v, p = jnp.split(proj, [ATTN_WIDTH, 2 * ATTN_WIDTH, 3 * ATTN_WIDTH], axis=-1)
        q = partial_rotary(q.reshape(B, S, N_ATTN_HEADS, HEAD_DIM), pos)
        k = partial_rotary(k.reshape(B, S, N_ATTN_HEADS, HEAD_DIM), pos)
        v = v.reshape(B, S, N_ATTN_HEADS, HEAD_DIM)

        outs, lses = [], []
        for g, (window, dil) in enumerate(DILATION_PAIRS):
            sl = slice(g * HEADS_PER_GROUP, (g + 1) * HEADS_PER_GROUP)
            qg = q[:, :, sl].transpose(0, 2, 1, 3)
            kg = k[:, :, sl].transpose(0, 2, 1, 3)
            vg = v[:, :, sl].transpose(0, 2, 1, 3)
            o, l = dilated_attention(qg, kg, vg, dil, window // 2 // dil)
            outs.append(o)
            lses.append(l)
        alpha = jax.nn.softmax(jnp.stack(lses, axis=0), axis=0)
        attn = jnp.sum(alpha[..., None] * jnp.stack(outs, axis=0), axis=0)
        attn = attn.transpose(0, 2, 1, 3).reshape(B, S, ATTN_OUT_WIDTH).astype(h.dtype)
        y_a = attn @ w_attn_out

        pooled = multiscale_pool(p).reshape(B, S, len(POOL_WINDOWS), POOL_GROUP_WIDTH)
        pm = jnp.einsum('bsgc,gcd->bsgd', pooled, pool_w).reshape(B, S, POOL_WIDTH) * pool_scale
        y_b = pm @ w_pool_out

        g_a, g_b = jnp.split(jax.nn.sigmoid(u @ w_gate), 2, axis=-1)
        h = h + (g_a * y_a + g_b * y_b) @ w_out

        z = rmsnorm(h, norm_ffn_g)
        a, b = jnp.split(z @ w_up, 2, axis=-1)
        a = depthwise_conv3(a, conv_w, conv_b)
        h = h + (jax.nn.gelu(a, approximate=False) * b) @ w_down
    return rmsnorm(h, norm_final_g)
```

```python
import functools

import numpy as np
import jax
import jax.numpy as jnp
from jax import lax
from jax.experimental import pallas as pl
from jax.experimental.pallas import tpu as pltpu

F32 = jnp.float32
BF16 = jnp.bfloat16

D_MODEL = 2048
HEAD_DIM = 128
DILATION_PAIRS = ((128, 1), (512, 4), (2048, 16))
N_GROUPS = len(DILATION_PAIRS)
HEADS_PER_GROUP = 4
N_ATTN_HEADS = N_GROUPS * HEADS_PER_GROUP
ATTN_WIDTH = N_ATTN_HEADS * HEAD_DIM
ATTN_OUT_WIDTH = HEADS_PER_GROUP * HEAD_DIM
ROT_DIM = HEAD_DIM // 4
ROT_HALF = ROT_DIM // 2
ROPE_THETA = 500000.0
POOL_WINDOWS = (2, 4, 8, 16)
POOL_WIDTH = 512
POOL_GROUP_WIDTH = 128
D_FF = 3 * D_MODEL
NORM_EPS = 1e-6
NEG_INF = -1e30
RADIUS = 64
assert all(w // 2 // d == RADIUS for w, d in DILATION_PAIRS)

LANES = 128
BF16_SUBLANES = 16
VMEM_LIMIT_BYTES = 56 * 1024 * 1024

GATE_COLS = 2 * D_MODEL
Q_COL = GATE_COLS
K_COL = Q_COL + ATTN_WIDTH
V_COL = K_COL + ATTN_WIDTH
P_COL = V_COL + ATTN_WIDTH
PROJ_COLS = P_COL + POOL_WIDTH

TM_IN, TN_IN = 1024, 1024
ROT_TILE_LO = GATE_COLS // TN_IN
ROT_TILE_HI = V_COL // TN_IN
assert GATE_COLS % TN_IN == 0 and V_COL % TN_IN == 0 and PROJ_COLS % TN_IN == 0
ATTN_QB = 128
ATTN_KW = ATTN_QB + 2 * RADIUS
ATTN_UNROLL = 4
TM_MIX = 256
POOL_HALO = BF16_SUBLANES
TM_FFN, TF_FFN = 512, 512
CONV_HALO = BF16_SUBLANES


def _rmsnorm(x, g):
    return x * lax.rsqrt(jnp.mean(x * x, axis=-1, keepdims=True) + NORM_EPS) * g


def _inproj_body(x_ref, g_ref, w_ref, cos_ref, sin_ref, o_ref, u_ref):
    j = pl.program_id(1)

    @pl.when(j == 0)
    def _():
        u_ref[...] = _rmsnorm(x_ref[...], g_ref[...]).astype(BF16)

    acc = jnp.dot(u_ref[...], w_ref[...], preferred_element_type=F32)

    @pl.when(j < ROT_TILE_LO)
    def _():
        o_ref[...] = jax.nn.sigmoid(acc).astype(BF16)

    @pl.when((j >= ROT_TILE_LO) & (j < ROT_TILE_HI))
    def _():
        c = cos_ref[...]
        s = sin_ref[...]
        for hd in range(TN_IN // HEAD_DIM):
            sl = slice(hd * HEAD_DIM, (hd + 1) * HEAD_DIM)
            blk = acc[:, sl]
            o_ref[:, sl] = (blk * c + pltpu.roll(blk, HEAD_DIM // 2, axis=1) * s).astype(BF16)

    @pl.when(j >= ROT_TILE_HI)
    def _():
        o_ref[...] = acc.astype(BF16)


def _inproj(x2, g, w_cat, cos_t, sin_t, seq):
    t = x2.shape[0]
    pos_tiles = seq // TM_IN
    return pl.pallas_call(
        _inproj_body,
        grid=(t // TM_IN, PROJ_COLS // TN_IN),
        in_specs=[
            pl.BlockSpec((TM_IN, D_MODEL), lambda i, j: (i, 0)),
            pl.BlockSpec((1, D_MODEL), lambda i, j: (0, 0)),
            pl.BlockSpec((D_MODEL, TN_IN), lambda i, j: (0, j)),
            pl.BlockSpec((TM_IN, HEAD_DIM), lambda i, j: (i % pos_tiles, 0)),
            pl.BlockSpec((TM_IN, HEAD_DIM), lambda i, j: (i % pos_tiles, 0)),
        ],
        out_specs=pl.BlockSpec((TM_IN, TN_IN), lambda i, j: (i, j)),
        out_shape=jax.ShapeDtypeStruct((t, PROJ_COLS), BF16),
        scratch_shapes=[pltpu.VMEM((TM_IN, D_MODEL), BF16)],
        compiler_params=pltpu.CompilerParams(
            dimension_semantics=("parallel", "arbitrary"), vmem_limit_bytes=VMEM_LIMIT_BYTES),
        name="inproj",
    )(x2, g, w_cat, cos_t, sin_t)


def _attn_group(gi, d, seq, q_ref, k_ref, v_ref, stage, qd, kd, vd, og, lg):
    L = seq // d
    LP = L + 2 * RADIUS
    nqb = L // ATTN_QB
    chunk = min(512, L)

    def deinterleave(src_ref, dst_ref, padded):
        for c in range(seq // 512):
            sl = slice(c * 512, (c + 1) * 512)
            stage[sl, :] = src_ref[sl, :].astype(F32)
        for r in range(d):
            base = r * LP + RADIUS if padded else r * L
            for c in range(L // chunk):
                vals = stage[pl.ds(r + c * chunk * d, chunk, stride=d), :]
                dst_ref[base + c * chunk: base + (c + 1) * chunk, :] = vals.astype(BF16)

    zero_pad = jnp.zeros((RADIUS, HEAD_DIM), BF16)
    for r in range(d):
        for dst in (kd, vd):
            dst[r * LP: r * LP + RADIUS, :] = zero_pad
            dst[r * LP + RADIUS + L: (r + 1) * LP, :] = zero_pad
    if d == 1:
        q_src = q_ref
        kd[RADIUS: RADIUS + seq, :] = k_ref[...]
        vd[RADIUS: RADIUS + seq, :] = v_ref[...]
    else:
        q_src = qd
        deinterleave(q_ref, qd, False)
        deinterleave(k_ref, kd, True)
        deinterleave(v_ref, vd, True)

    row = lax.broadcasted_iota(jnp.int32, (ATTN_QB, ATTN_KW), 0)
    col = lax.broadcasted_iota(jnp.int32, (ATTN_QB, ATTN_KW), 1)
    band = (col >= row) & (col <= row + 2 * RADIUS)
    scale = HEAD_DIM ** -0.5

    def body(blk, carry):
        r = blk // nqb
        qb = blk % nqb
        q0 = pl.multiple_of(r * L + qb * ATTN_QB, ATTN_QB)
        k0 = pl.multiple_of(r * LP + qb * ATTN_QB, ATTN_QB)
        q = q_src[pl.ds(q0, ATTN_QB), :]
        kw = kd[pl.ds(k0, ATTN_KW), :]
        vw = vd[pl.ds(k0, ATTN_KW), :]
        s = lax.dot_general(q, kw, (((1,), (1,)), ((), ())), preferred_element_type=F32) * scale
        lo = RADIUS - qb * ATTN_QB
        valid = band & (col >= lo) & (col < lo + L)
        s = jnp.where(valid, s, NEG_INF)
        m = jnp.max(s, axis=-1, keepdims=True)
        p = jnp.where(valid, jnp.exp(s - m), 0.0)
        den = jnp.sum(p, axis=-1, keepdims=True)
        pv = jnp.dot(p.astype(BF16), vw, preferred_element_type=F32)
        o = pv / den
        lse = jnp.broadcast_to(m + jnp.log(den), (ATTN_QB, HEAD_DIM))
        if d == 1:
            t0 = pl.multiple_of(qb * ATTN_QB, ATTN_QB)
            og[gi][pl.ds(t0, ATTN_QB), :] = o
            lg[gi][pl.ds(t0, ATTN_QB), :] = lse
        else:
            t0 = qb * (ATTN_QB * d) + r
            og[gi][pl.ds(t0, ATTN_QB, stride=d), :] = o
            lg[gi][pl.ds(t0, ATTN_QB, stride=d), :] = lse
        return carry

    lax.fori_loop(0, seq // ATTN_QB, body, 0, unroll=ATTN_UNROLL)


def _attn_body(seq, q_ref, k_ref, v_ref, o_ref, stage, qd, kd, vd,
               og0, og1, og2, lg0, lg1, lg2):
    g = pl.program_id(2)
    og = (og0, og1, og2)
    lg = (lg0, lg1, lg2)
    for gi, (_, d) in enumerate(DILATION_PAIRS):
        @pl.when(g == gi)
        def _(gi=gi, d=d):
            _attn_group(gi, d, seq, q_ref, k_ref, v_ref, stage, qd, kd, vd, og, lg)

    @pl.when(g == N_GROUPS - 1)
    def _():
        for c in range(seq // 256):
            sl = slice(c * 256, (c + 1) * 256)
            l0, l1, l2 = lg0[sl, :], lg1[sl, :], lg2[sl, :]
            mx = jnp.maximum(jnp.maximum(l0, l1), l2)
            e0, e1, e2 = jnp.exp(l0 - mx), jnp.exp(l1 - mx), jnp.exp(l2 - mx)
            num = e0 * og0[sl, :] + e1 * og1[sl, :] + e2 * og2[sl, :]
            o_ref[sl, :] = (num / (e0 + e1 + e2)).astype(BF16)


def _attention(proj3):
    b, seq, _ = proj3.shape
    max_d = max(d for _, d in DILATION_PAIRS)

    def head_spec(col0):
        blk0 = col0 // HEAD_DIM
        return pl.BlockSpec((None, seq, HEAD_DIM),
                            lambda bi, hs, g: (bi, 0, blk0 + g * HEADS_PER_GROUP + hs))

    tok = pltpu.VMEM((seq, HEAD_DIM), F32)
    return pl.pallas_call(
        functools.partial(_attn_body, seq),
        grid=(b, HEADS_PER_GROUP, N_GROUPS),
        in_specs=[head_spec(Q_COL), head_spec(K_COL), head_spec(V_COL)],
        out_specs=pl.BlockSpec((None, seq, HEAD_DIM), lambda bi, hs, g: (bi, 0, hs)),
        out_shape=jax.ShapeDtypeStruct((b, seq, ATTN_OUT_WIDTH), BF16),
        scratch_shapes=[
            tok,
            pltpu.VMEM((seq, HEAD_DIM), BF16),
            pltpu.VMEM((seq + 2 * RADIUS * max_d, HEAD_DIM), BF16),
            pltpu.VMEM((seq + 2 * RADIUS * max_d, HEAD_DIM), BF16),
            tok, tok, tok, tok, tok, tok,
        ],
        compiler_params=pltpu.CompilerParams(
            dimension_semantics=("parallel", "parallel", "arbitrary"),
            vmem_limit_bytes=VMEM_LIMIT_BYTES),
        name="attn",
    )(proj3, proj3, proj3)


def _mix_body(seq, attn_ref, p_ref, pprev_ref, pnext_ref, ga_ref, gb_ref, x_ref,
              wa_ref, pw_ref, ps_ref, wp_ref, wo_ref, gn_ref, h_ref, z_ref, pad, pm):
    i = pl.program_id(0)
    tiles_per_seq = seq // TM_MIX
    ti = i % tiles_per_seq
    pad[0:POOL_HALO, :] = jnp.where(ti == 0, 0.0, pprev_ref[...].astype(F32))
    pad[POOL_HALO: POOL_HALO + TM_MIX, :] = p_ref[...].astype(F32)
    pad[POOL_HALO + TM_MIX:, :] = jnp.where(ti == tiles_per_seq - 1, 0.0, pnext_ref[...].astype(F32))

    t = ti * TM_MIX + lax.broadcasted_iota(jnp.int32, (TM_MIX, POOL_GROUP_WIDTH), 0)
    for g, w in enumerate(POOL_WINDOWS):
        cs = slice(g * POOL_GROUP_WIDTH, (g + 1) * POOL_GROUP_WIDTH)
        tot = pad[POOL_HALO - w // 2: POOL_HALO - w // 2 + TM_MIX, cs]
        for j in range(-w // 2 + 1, w // 2):
            tot = tot + pad[POOL_HALO + j: POOL_HALO + j + TM_MIX, cs]
        cnt = (jnp.minimum(t + w // 2, seq) - jnp.maximum(t - w // 2, 0)).astype(F32)
        pooled = tot / cnt - pad[POOL_HALO: POOL_HALO + TM_MIX, cs]
        pmg = jnp.dot(pooled.astype(BF16), pw_ref[g], preferred_element_type=F32) * ps_ref[:, cs]
        pm[:, cs] = pmg.astype(BF16)

    y_a = jnp.dot(attn_ref[...], wa_ref[...], preferred_element_type=F32)
    y_b = jnp.dot(pm[...], wp_ref[...], preferred_element_type=F32)
    mixed = ga_ref[...].astype(F32) * y_a + gb_ref[...].astype(F32) * y_b
    h = x_ref[...] + jnp.dot(mixed.astype(BF16), wo_ref[...], preferred_element_type=F32)
    h_ref[...] = h
    z_ref[...] = _rmsnorm(h, gn_ref[...]).astype(BF16)


def _mix(attn2, proj2, x2, wa, pw, ps, wp, wo, gn, seq):
    t = x2.shape[0]
    halo_blocks = TM_MIX // POOL_HALO
    n_halo = t // POOL_HALO
    p_blk = P_COL // POOL_WIDTH
    const = dict(pipeline_mode=pl.Buffered(1))
    return pl.pallas_call(
        functools.partial(_mix_body, seq),
        grid=(t // TM_MIX,),
        in_specs=[
            pl.BlockSpec((TM_MIX, ATTN_OUT_WIDTH), lambda i: (i, 0)),
            pl.BlockSpec((TM_MIX, POOL_WIDTH), lambda i: (i, p_blk)),
            pl.BlockSpec((POOL_HALO, POOL_WIDTH),
                         lambda i: (jnp.maximum(i * halo_blocks - 1, 0), p_blk)),
            pl.BlockSpec((POOL_HALO, POOL_WIDTH),
                         lambda i: (jnp.minimum((i + 1) * halo_blocks, n_halo - 1), p_blk)),
            pl.BlockSpec((TM_MIX, D_MODEL), lambda i: (i, 0)),
            pl.BlockSpec((TM_MIX, D_MODEL), lambda i: (i, 1)),
            pl.BlockSpec((TM_MIX, D_MODEL), lambda i: (i, 0)),
            pl.BlockSpec((ATTN_OUT_WIDTH, D_MODEL), lambda i: (0, 0), **const),
            pl.BlockSpec((len(POOL_WINDOWS), POOL_GROUP_WIDTH, POOL_GROUP_WIDTH),
                         lambda i: (0, 0, 0), **const),
            pl.BlockSpec((1, POOL_WIDTH), lambda i: (0, 0)),
            pl.BlockSpec((POOL_WIDTH, D_MODEL), lambda i: (0, 0), **const),
            pl.BlockSpec((D_MODEL, D_MODEL), lambda i: (0, 0), **const),
            pl.BlockSpec((1, D_MODEL), lambda i: (0, 0)),
        ],
        out_specs=[
            pl.BlockSpec((TM_MIX, D_MODEL), lambda i: (i, 0)),
            pl.BlockSpec((TM_MIX, D_MODEL), lambda i: (i, 0)),
        ],
        out_shape=[
            jax.ShapeDtypeStruct((t, D_MODEL), F32),
            jax.ShapeDtypeStruct((t, D_MODEL), BF16),
        ],
        scratch_shapes=[
            pltpu.VMEM((TM_MIX + 2 * POOL_HALO, POOL_WIDTH), F32),
            pltpu.VMEM((TM_MIX, POOL_WIDTH), BF16),
        ],
        compiler_params=pltpu.CompilerParams(
            dimension_semantics=("parallel",), vmem_limit_bytes=VMEM_LIMIT_BYTES),
        name="mix",
    )(attn2, proj2, proj2, proj2, proj2, proj2, x2, wa, pw, ps, wp, wo, gn)


def _ffn_body(seq, z_ref, zprev_ref, znext_ref, h_ref, wa_ref, wb_ref, cw_ref, cb_ref, wd_ref,
              gn_ref, o_ref, zext, aext, acc):
    i = pl.program_id(0)
    f = pl.program_id(1)
    tiles_per_seq = seq // TM_FFN
    ti = i % tiles_per_seq

    @pl.when(f == 0)
    def _():
        zero_rows = jnp.zeros((CONV_HALO, D_MODEL), BF16)
        zext[0:CONV_HALO, :] = jnp.where(ti == 0, zero_rows, zprev_ref[...])
        zext[CONV_HALO: CONV_HALO + TM_FFN, :] = z_ref[...]
        zext[CONV_HALO + TM_FFN:, :] = jnp.where(ti == tiles_per_seq - 1, zero_rows, znext_ref[...])
        acc[...] = jnp.zeros_like(acc)

    aext[...] = jnp.dot(zext[...], wa_ref[...], preferred_element_type=F32)
    b = jnp.dot(zext[CONV_HALO: CONV_HALO + TM_FFN, :], wb_ref[...], preferred_element_type=F32)
    cw = cw_ref[...]
    a = (aext[CONV_HALO - 1: CONV_HALO - 1 + TM_FFN, :] * cw[0:1, :]
         + aext[CONV_HALO: CONV_HALO + TM_FFN, :] * cw[1:2, :]
         + aext[CONV_HALO + 1: CONV_HALO + 1 + TM_FFN, :] * cw[2:3, :]
         + cb_ref[...])
    gelu = 0.5 * a * (1.0 + lax.erf(a * np.float32(np.sqrt(0.5))))
    acc[...] += jnp.dot((gelu * b).astype(BF16), wd_ref[...], preferred_element_type=F32)

    @pl.when(f == pl.num_programs(1) - 1)
    def _():
        o_ref[...] = _rmsnorm(h_ref[...] + acc[...], gn_ref[...])


def _ffn(z2, h2, w_up, cw, cb, w_down, gn, seq):
    t = z2.shape[0]
    halo_blocks = TM_FFN // CONV_HALO
    n_halo = t // CONV_HALO
    nf = D_FF // TF_FFN
    return pl.pallas_call(
        functools.partial(_ffn_body, seq),
        grid=(t // TM_FFN, nf),
        in_specs=[
            pl.BlockSpec((TM_FFN, D_MODEL), lambda i, f: (i, 0)),
            pl.BlockSpec((CONV_HALO, D_MODEL), lambda i, f: (jnp.maximum(i * halo_blocks - 1, 0), 0)),
            pl.BlockSpec((CONV_HALO, D_MODEL),
                         lambda i, f: (jnp.minimum((i + 1) * halo_blocks, n_halo - 1), 0)),
            pl.BlockSpec((TM_FFN, D_MODEL), lambda i, f: (i, 0)),
            pl.BlockSpec((D_MODEL, TF_FFN), lambda i, f: (0, f)),
            pl.BlockSpec((D_MODEL, TF_FFN), lambda i, f: (0, nf + f)),
            pl.BlockSpec((3, TF_FFN), lambda i, f: (0, f)),
            pl.BlockSpec((1, TF_FFN), lambda i, f: (0, f)),
            pl.BlockSpec((TF_FFN, D_MODEL), lambda i, f: (f, 0)),
            pl.BlockSpec((1, D_MODEL), lambda i, f: (0, 0)),
        ],
        out_specs=pl.BlockSpec((TM_FFN, D_MODEL), lambda i, f: (i, 0)),
        out_shape=jax.ShapeDtypeStruct((t, D_MODEL), F32),
        scratch_shapes=[
            pltpu.VMEM((TM_FFN + 2 * CONV_HALO, D_MODEL), BF16),
            pltpu.VMEM((TM_FFN + 2 * CONV_HALO, TF_FFN), F32),
            pltpu.VMEM((TM_FFN, D_MODEL), F32),
        ],
        compiler_params=pltpu.CompilerParams(
            dimension_semantics=("parallel", "arbitrary"), vmem_limit_bytes=VMEM_LIMIT_BYTES),
        name="ffn",
    )(z2, z2, z2, h2, w_up, w_up, cw, cb, w_down, gn)


def _rotary_layout(w):
    d = w.shape[0]
    wh = w.reshape(d, -1, HEAD_DIM)
    keep = HEAD_DIM // 2 - ROT_HALF
    out = jnp.concatenate([
        wh[..., :ROT_HALF],
        wh[..., ROT_DIM: ROT_DIM + keep],
        wh[..., ROT_HALF: ROT_DIM],
        wh[..., ROT_DIM + keep:],
    ], axis=-1)
    return out.reshape(d, -1)


def _rotary_tables(seq):
    inv_freq = ROPE_THETA ** (-jnp.arange(0, ROT_DIM, 2, dtype=F32) / ROT_DIM)
    ang = jnp.arange(seq).astype(F32)[:, None] * inv_freq[None, :]
    cos, sin = jnp.cos(ang), jnp.sin(ang)
    half = HEAD_DIM // 2
    ones = jnp.ones((seq, half - ROT_HALF), F32)
    zeros = jnp.zeros((seq, half - ROT_HALF), F32)
    cos_t = jnp.concatenate([cos, ones, cos, ones], axis=-1)
    sin_t = jnp.concatenate([-sin, zeros, sin, zeros], axis=-1)
    return cos_t, sin_t


def kernel(x, norm_mix_g, w_in, w_attn_out, pool_w, pool_scale, w_pool_out, w_gate, w_out,
           norm_ffn_g, w_up, conv_w, conv_b, w_down, norm_final_g):
    b, seq, d = x.shape
    assert d == D_MODEL and seq % TM_IN == 0 and seq % TM_FFN == 0 and seq % TM_MIX == 0
    assert all(seq % (dil * ATTN_QB) == 0 for _, dil in DILATION_PAIRS)
    t = b * seq
    x2 = x.reshape(t, d)

    w_cat = jnp.concatenate(
        [w_gate, _rotary_layout(w_in[:, :2 * ATTN_WIDTH]), w_in[:, 2 * ATTN_WIDTH:]], axis=1
    ).astype(BF16)
    cos_t, sin_t = _rotary_tables(seq)

    proj = _inproj(x2, norm_mix_g.reshape(1, d), w_cat, cos_t, sin_t, seq)
    attn = _attention(proj.reshape(b, seq, PROJ_COLS))
    h, z = _mix(attn.reshape(t, ATTN_OUT_WIDTH), proj, x2,
                w_attn_out.astype(BF16), pool_w.astype(BF16), pool_scale.reshape(1, POOL_WIDTH),
                w_pool_out.astype(BF16), w_out.astype(BF16), norm_ffn_g.reshape(1, d), seq)
    out = _ffn(z, h, w_up.astype(BF16), conv_w, conv_b.reshape(1, D_FF), w_down.astype(BF16),
               norm_final_g.reshape(1, d), seq)
    return out.reshape(b, seq, d)
```

```python
import functools

import numpy as np
import jax
import jax.numpy as jnp
from jax import lax
from jax.experimental import pallas as pl
from jax.experimental.pallas import tpu as pltpu

F32 = jnp.float32
BF16 = jnp.bfloat16

D_MODEL = 2048
HEAD_DIM = 128
DILATION_PAIRS = ((128, 1), (512, 4), (2048, 16))
N_GROUPS = len(DILATION_PAIRS)
HEADS_PER_GROUP = 4
N_ATTN_HEADS = N_GROUPS * HEADS_PER_GROUP
ATTN_WIDTH = N_ATTN_HEADS * HEAD_DIM
ATTN_OUT_WIDTH = HEADS_PER_GROUP * HEAD_DIM
ROT_DIM = HEAD_DIM // 4
ROT_HALF = ROT_DIM // 2
ROPE_THETA = 500000.0
POOL_WINDOWS = (2, 4, 8, 16)
POOL_WIDTH = 512
POOL_GROUP_WIDTH = 128
D_FF = 3 * D_MODEL
NORM_EPS = 1e-6
NEG_INF = -1e30
RADIUS = 64
assert all(w // 2 // d == RADIUS for w, d in DILATION_PAIRS)

LANES = 128
BF16_SUBLANES = 16
VMEM_LIMIT_BYTES = 56 * 1024 * 1024

GATE_COLS = 2 * D_MODEL
Q_COL = GATE_COLS
K_COL = Q_COL + ATTN_WIDTH
V_COL = K_COL + ATTN_WIDTH
P_COL = V_COL + ATTN_WIDTH
PROJ_COLS = P_COL + POOL_WIDTH

TM_IN, TN_IN = 1024, 1024
ROT_TILE_LO = GATE_COLS // TN_IN
ROT_TILE_HI = V_COL // TN_IN
assert GATE_COLS % TN_IN == 0 and V_COL % TN_IN == 0 and PROJ_COLS % TN_IN == 0
ATTN_QB = 128
ATTN_KW = ATTN_QB + 2 * RADIUS
ATTN_UNROLL = 8
ATTN_PERM_ROWS = 256
TM_MIX = 256
POOL_HALO = BF16_SUBLANES
TM_FFN, TF_FFN = 512, 1024
CONV_HALO = BF16_SUBLANES


def _rmsnorm(x, g):
    return x * lax.rsqrt(jnp.mean(x * x, axis=-1, keepdims=True) + NORM_EPS) * g


def _inproj_body(x_ref, g_ref, wg_ref, wi_ref, cos_ref, sina_ref, sinb_ref, o_ref, u_ref):
    j = pl.program_id(1)

    @pl.when(j == 0)
    def _():
        u_ref[...] = _rmsnorm(x_ref[...], g_ref[...]).astype(BF16)

    @pl.when(j < ROT_TILE_LO)
    def _():
        acc = jnp.dot(u_ref[...], wg_ref[...], preferred_element_type=F32)
        o_ref[...] = jax.nn.sigmoid(acc).astype(BF16)

    @pl.when((j >= ROT_TILE_LO) & (j < ROT_TILE_HI))
    def _():
        acc = jnp.dot(u_ref[...], wi_ref[...], preferred_element_type=F32)
        c = cos_ref[...]
        sa = sina_ref[...]
        sb = sinb_ref[...]
        for hd in range(TN_IN // HEAD_DIM):
            sl = slice(hd * HEAD_DIM, (hd + 1) * HEAD_DIM)
            blk = acc[:, sl]
            up = pltpu.roll(blk, HEAD_DIM - ROT_HALF, axis=1)
            dn = pltpu.roll(blk, ROT_HALF, axis=1)
            o_ref[:, sl] = (blk * c + up * sa + dn * sb).astype(BF16)

    @pl.when(j >= ROT_TILE_HI)
    def _():
        acc = jnp.dot(u_ref[...], wi_ref[...], preferred_element_type=F32)
        o_ref[...] = acc.astype(BF16)


def _inproj(x2, g, w_gate, w_in, cos_t, sina_t, sinb_t, seq):
    t = x2.shape[0]
    pos_tiles = seq // TM_IN
    gate_tiles = GATE_COLS // TN_IN
    table_spec = pl.BlockSpec((TM_IN, HEAD_DIM), lambda i, j: (i % pos_tiles, 0))
    return pl.pallas_call(
        _inproj_body,
        grid=(t // TM_IN, PROJ_COLS // TN_IN),
        in_specs=[
            pl.BlockSpec((TM_IN, D_MODEL), lambda i, j: (i, 0)),
            pl.BlockSpec((1, D_MODEL), lambda i, j: (0, 0)),
            pl.BlockSpec((D_MODEL, TN_IN), lambda i, j: (0, jnp.minimum(j, gate_tiles - 1))),
            pl.BlockSpec((D_MODEL, TN_IN), lambda i, j: (0, jnp.maximum(j - gate_tiles, 0))),
            table_spec, table_spec, table_spec,
        ],
        out_specs=pl.BlockSpec((TM_IN, TN_IN), lambda i, j: (i, j)),
        out_shape=jax.ShapeDtypeStruct((t, PROJ_COLS), BF16),
        scratch_shapes=[pltpu.VMEM((TM_IN, D_MODEL), BF16)],
        compiler_params=pltpu.CompilerParams(
            dimension_semantics=("parallel", "arbitrary"), vmem_limit_bytes=VMEM_LIMIT_BYTES),
        name="inproj",
    )(x2, g, w_gate, w_in, cos_t, sina_t, sinb_t)


def _perm_matrices():
    mats = []
    for _, d in DILATION_PAIRS:
        seg = ATTN_PERM_ROWS // d
        p = np.zeros((ATTN_PERM_ROWS, ATTN_PERM_ROWS), np.float32)
        for r in range(d):
            for l in range(seg):
                p[r * seg + l, l * d + r] = 1.0
        mats.append(p)
    return jnp.asarray(np.stack(mats), dtype=BF16)


def _attn_group(gi, d, seq, q_ref, k_ref, v_ref, perm_ref, qd, kd, vd, mask, bias, og, lg):
    L = seq // d
    LP = L + 2 * RADIUS
    nqb = L // ATTN_QB
    seg = ATTN_PERM_ROWS // d

    def deinterleave(src_ref, dst_ref, padded):
        for c in range(seq // ATTN_PERM_ROWS):
            x = src_ref[c * ATTN_PERM_ROWS: (c + 1) * ATTN_PERM_ROWS, :]
            y = jnp.dot(perm_ref[gi], x, preferred_element_type=F32).astype(BF16)
            for r in range(d):
                base = (r * LP + RADIUS if padded else r * L) + c * seg
                dst_ref[base: base + seg, :] = y[r * seg: (r + 1) * seg, :]

    zero_pad = jnp.zeros((RADIUS, HEAD_DIM), BF16)
    for r in range(d):
        for dst in (kd, vd):
            dst[r * LP: r * LP + RADIUS, :] = zero_pad
            dst[r * LP + RADIUS + L: (r + 1) * LP, :] = zero_pad
    if d == 1:
        q_src = q_ref
        kd[RADIUS: RADIUS + seq, :] = k_ref[...]
        vd[RADIUS: RADIUS + seq, :] = v_ref[...]
    else:
        q_src = qd
        deinterleave(q_ref, qd, False)
        deinterleave(k_ref, kd, True)
        deinterleave(v_ref, vd, True)

    scale = HEAD_DIM ** -0.5

    def body(blk, carry):
        r = blk // nqb
        qb = blk % nqb
        q0 = pl.multiple_of(r * L + qb * ATTN_QB, ATTN_QB)
        k0 = pl.multiple_of(r * LP + qb * ATTN_QB, ATTN_QB)
        q = q_src[pl.ds(q0, ATTN_QB), :]
        kw = kd[pl.ds(k0, ATTN_KW), :]
        vw = vd[pl.ds(k0, ATTN_KW), :]
        kind = jnp.where(qb == 0, 1, jnp.where(qb == nqb - 1, 2, 0))
        s = lax.dot_general(q, kw, (((1,), (1,)), ((), ())), preferred_element_type=F32)
        s = s * scale + bias[kind]
        m = jnp.max(s, axis=-1, keepdims=True)
        p = jnp.exp(s - m) * mask[kind]
        den = jnp.sum(p, axis=-1, keepdims=True)
        pv = jnp.dot(p.astype(BF16), vw, preferred_element_type=F32)
        o = pv / den
        lse = jnp.broadcast_to(m + jnp.log(den), (ATTN_QB, HEAD_DIM))
        if d == 1:
            t0 = pl.multiple_of(qb * ATTN_QB, ATTN_QB)
            og[gi][pl.ds(t0, ATTN_QB), :] = o
            lg[gi][pl.ds(t0, ATTN_QB), :] = lse
        else:
            t0 = qb * (ATTN_QB * d) + r
            og[gi][pl.ds(t0, ATTN_QB, stride=d), :] = o
            lg[gi][pl.ds(t0, ATTN_QB, stride=d), :] = lse
        return carry

    lax.fori_loop(0, seq // ATTN_QB, body, 0, unroll=ATTN_UNROLL)


def _attn_body(seq, q_ref, k_ref, v_ref, perm_ref, o_ref, qd, kd, vd, mask, bias,
               og0, og1, og2, lg0, lg1, lg2):
    g = pl.program_id(2)
    og = (og0, og1, og2)
    lg = (lg0, lg1, lg2)

    row = lax.broadcasted_iota(jnp.int32, (ATTN_QB, ATTN_KW), 0)
    col = lax.broadcasted_iota(jnp.int32, (ATTN_QB, ATTN_KW), 1)
    band = (col >= row) & (col <= row + 2 * RADIUS)
    for kind, valid in enumerate((band, band & (col >= RADIUS), band & (col < ATTN_QB + RADIUS))):
        mask[kind] = jnp.where(valid, 1.0, 0.0)
        bias[kind] = jnp.where(valid, 0.0, NEG_INF)

    for gi, (_, d) in enumerate(DILATION_PAIRS):
        @pl.when(g == gi)
        def _(gi=gi, d=d):
            _attn_group(gi, d, seq, q_ref, k_ref, v_ref, perm_ref, qd, kd, vd, mask, bias, og, lg)

    @pl.when(g == N_GROUPS - 1)
    def _():
        for c in range(seq // 256):
            sl = slice(c * 256, (c + 1) * 256)
            l0, l1, l2 = lg0[sl, :], lg1[sl, :], lg2[sl, :]
            mx = jnp.maximum(jnp.maximum(l0, l1), l2)
            e0, e1, e2 = jnp.exp(l0 - mx), jnp.exp(l1 - mx), jnp.exp(l2 - mx)
            num = e0 * og0[sl, :] + e1 * og1[sl, :] + e2 * og2[sl, :]
            o_ref[sl, :] = (num / (e0 + e1 + e2)).astype(BF16)


def _attention(proj3):
    b, seq, _ = proj3.shape
    max_d = max(d for _, d in DILATION_PAIRS)

    def head_spec(col0):
        blk0 = col0 // HEAD_DIM
        return pl.BlockSpec((None, seq, HEAD_DIM),
                            lambda bi, hs, g: (bi, 0, blk0 + g * HEADS_PER_GROUP + hs))

    tok = pltpu.VMEM((seq, HEAD_DIM), F32)
    win = pltpu.VMEM((3, ATTN_QB, ATTN_KW), F32)
    return pl.pallas_call(
        functools.partial(_attn_body, seq),
        grid=(b, HEADS_PER_GROUP, N_GROUPS),
        in_specs=[head_spec(Q_COL), head_spec(K_COL), head_spec(V_COL),
                  pl.BlockSpec((N_GROUPS, ATTN_PERM_ROWS, ATTN_PERM_ROWS),
                               lambda bi, hs, g: (0, 0, 0))],
        out_specs=pl.BlockSpec((None, seq, HEAD_DIM), lambda bi, hs, g: (bi, 0, hs)),
        out_shape=jax.ShapeDtypeStruct((b, seq, ATTN_OUT_WIDTH), BF16),
        scratch_shapes=[
            pltpu.VMEM((seq, HEAD_DIM), BF16),
            pltpu.VMEM((seq + 2 * RADIUS * max_d, HEAD_DIM), BF16),
            pltpu.VMEM((seq + 2 * RADIUS * max_d, HEAD_DIM), BF16),
            win, win,
            tok, tok, tok, tok, tok, tok,
        ],
        compiler_params=pltpu.CompilerParams(
            dimension_semantics=("parallel", "parallel", "arbitrary"),
            vmem_limit_bytes=VMEM_LIMIT_BYTES),
        name="attn",
    )(proj3, proj3, proj3, _perm_matrices())


def _mix_body(seq, attn_ref, p_ref, pprev_ref, pnext_ref, ga_ref, gb_ref, x_ref,
              wa_ref, pw_ref, ps_ref, wp_ref, wo_ref, gn_ref, h_ref, z_ref, pad, pm):
    i = pl.program_id(0)
    tiles_per_seq = seq // TM_MIX
    ti = i % tiles_per_seq
    pad[0:POOL_HALO, :] = jnp.where(ti == 0, 0.0, pprev_ref[...].astype(F32))
    pad[POOL_HALO: POOL_HALO + TM_MIX, :] = p_ref[...].astype(F32)
    pad[POOL_HALO + TM_MIX:, :] = jnp.where(ti == tiles_per_seq - 1, 0.0, pnext_ref[...].astype(F32))

    t = ti * TM_MIX + lax.broadcasted_iota(jnp.int32, (TM_MIX, POOL_GROUP_WIDTH), 0)
    for g, w in enumerate(POOL_WINDOWS):
        cs = slice(g * POOL_GROUP_WIDTH, (g + 1) * POOL_GROUP_WIDTH)
        tot = pad[POOL_HALO - w // 2: POOL_HALO - w // 2 + TM_MIX, cs]
        for j in range(-w // 2 + 1, w // 2):
            tot = tot + pad[POOL_HALO + j: POOL_HALO + j + TM_MIX, cs]
        cnt = (jnp.minimum(t + w // 2, seq) - jnp.maximum(t - w // 2, 0)).astype(F32)
        pooled = tot / cnt - pad[POOL_HALO: POOL_HALO + TM_MIX, cs]
        pmg = jnp.dot(pooled.astype(BF16), pw_ref[g], preferred_element_type=F32) * ps_ref[:, cs]
        pm[:, cs] = pmg.astype(BF16)

    y_a = jnp.dot(attn_ref[...], wa_ref[...], preferred_element_type=F32)
    y_b = jnp.dot(pm[...], wp_ref[...], preferred_element_type=F32)
    mixed = ga_ref[...].astype(F32) * y_a + gb_ref[...].astype(F32) * y_b
    h = x_ref[...] + jnp.dot(mixed.astype(BF16), wo_ref[...], preferred_element_type=F32)
    h_ref[...] = h
    z_ref[...] = _rmsnorm(h, gn_ref[...]).astype(BF16)


def _mix(attn2, proj2, x2, wa, pw, ps, wp, wo, gn, seq):
    t = x2.shape[0]
    halo_blocks = TM_MIX // POOL_HALO
    n_halo = t // POOL_HALO
    p_blk = P_COL // POOL_WIDTH
    const = dict(pipeline_mode=pl.Buffered(1))
    return pl.pallas_call(
        functools.partial(_mix_body, seq),
        grid=(t // TM_MIX,),
        in_specs=[
            pl.BlockSpec((TM_MIX, ATTN_OUT_WIDTH), lambda i: (i, 0)),
            pl.BlockSpec((TM_MIX, POOL_WIDTH), lambda i: (i, p_blk)),
            pl.BlockSpec((POOL_HALO, POOL_WIDTH),
                         lambda i: (jnp.maximum(i * halo_blocks - 1, 0), p_blk)),
            pl.BlockSpec((POOL_HALO, POOL_WIDTH),
                         lambda i: (jnp.minimum((i + 1) * halo_blocks, n_halo - 1), p_blk)),
            pl.BlockSpec((TM_MIX, D_MODEL), lambda i: (i, 0)),
            pl.BlockSpec((TM_MIX, D_MODEL), lambda i: (i, 1)),
            pl.BlockSpec((TM_MIX, D_MODEL), lambda i: (i, 0)),
            pl.BlockSpec((ATTN_OUT_WIDTH, D_MODEL), lambda i: (0, 0), **const),
            pl.BlockSpec((len(POOL_WINDOWS), POOL_GROUP_WIDTH, POOL_GROUP_WIDTH),
                         lambda i: (0, 0, 0), **const),
            pl.BlockSpec((1, POOL_WIDTH), lambda i: (0, 0)),
            pl.BlockSpec((POOL_WIDTH, D_MODEL), lambda i: (0, 0), **const),
            pl.BlockSpec((D_MODEL, D_MODEL), lambda i: (0, 0), **const),
            pl.BlockSpec((1, D_MODEL), lambda i: (0, 0)),
        ],
        out_specs=[
            pl.BlockSpec((TM_MIX, D_MODEL), lambda i: (i, 0)),
            pl.BlockSpec((TM_MIX, D_MODEL), lambda i: (i, 0)),
        ],
        out_shape=[
            jax.ShapeDtypeStruct((t, D_MODEL), F32),
            jax.ShapeDtypeStruct((t, D_MODEL), BF16),
        ],
        scratch_shapes=[
            pltpu.VMEM((TM_MIX + 2 * POOL_HALO, POOL_WIDTH), F32),
            pltpu.VMEM((TM_MIX, POOL_WIDTH), BF16),
        ],
        compiler_params=pltpu.CompilerParams(
            dimension_semantics=("parallel",), vmem_limit_bytes=VMEM_LIMIT_BYTES),
        name="mix",
    )(attn2, proj2, proj2, proj2, proj2, proj2, x2, wa, pw, ps, wp, wo, gn)


def _ffn_body(seq, z_ref, zprev_ref, znext_ref, h_ref, wa_ref, wb_ref, cw_ref, cb_ref, wd_ref,
              gn_ref, o_ref, zext, aext):
    i = pl.program_id(0)
    f = pl.program_id(1)
    tiles_per_seq = seq // TM_FFN
    ti = i % tiles_per_seq

    @pl.when(f == 0)
    def _():
        zero_rows = jnp.zeros((CONV_HALO, D_MODEL), BF16)
        zext[0:CONV_HALO, :] = jnp.where(ti == 0, zero_rows, zprev_ref[...])
        zext[CONV_HALO: CONV_HALO + TM_FFN, :] = z_ref[...]
        zext[CONV_HALO + TM_FFN:, :] = jnp.where(ti == tiles_per_seq - 1, zero_rows, znext_ref[...])
        o_ref[...] = h_ref[...]

    aext[...] = jnp.dot(zext[...], wa_ref[...], preferred_element_type=F32)
    b = jnp.dot(zext[CONV_HALO: CONV_HALO + TM_FFN, :], wb_ref[...], preferred_element_type=F32)
    cw = cw_ref[...]
    a = (aext[CONV_HALO - 1: CONV_HALO - 1 + TM_FFN, :] * cw[0:1, :]
         + aext[CONV_HALO: CONV_HALO + TM_FFN, :] * cw[1:2, :]
         + aext[CONV_HALO + 1: CONV_HALO + 1 + TM_FFN, :] * cw[2:3, :]
         + cb_ref[...])
    gelu = 0.5 * a * (1.0 + lax.erf(a * np.float32(np.sqrt(0.5))))
    o_ref[...] += jnp.dot((gelu * b).astype(BF16), wd_ref[...], preferred_element_type=F32)

    @pl.when(f == pl.num_programs(1) - 1)
    def _():
        o_ref[...] = _rmsnorm(o_ref[...], gn_ref[...])


def _ffn(z2, h2, w_up, cw, cb, w_down, gn, seq):
    t = z2.shape[0]
    halo_blocks = TM_FFN // CONV_HALO
    n_halo = t // CONV_HALO
    nf = D_FF // TF_FFN
    return pl.pallas_call(
        functools.partial(_ffn_body, seq),
        grid=(t // TM_FFN, nf),
        in_specs=[
            pl.BlockSpec((TM_FFN, D_MODEL), lambda i, f: (i, 0)),
            pl.BlockSpec((CONV_HALO, D_MODEL), lambda i, f: (jnp.maximum(i * halo_blocks - 1, 0), 0)),
            pl.BlockSpec((CONV_HALO, D_MODEL),
                         lambda i, f: (jnp.minimum((i + 1) * halo_blocks, n_halo - 1), 0)),
            pl.BlockSpec((TM_FFN, D_MODEL), lambda i, f: (i, 0)),
            pl.BlockSpec((D_MODEL, TF_FFN), lambda i, f: (0, f)),
            pl.BlockSpec((D_MODEL, TF_FFN), lambda i, f: (0, nf + f)),
            pl.BlockSpec((3, TF_FFN), lambda i, f: (0, f)),
            pl.BlockSpec((1, TF_FFN), lambda i, f: (0, f)),
            pl.BlockSpec((TF_FFN, D_MODEL), lambda i, f: (f, 0)),
            pl.BlockSpec((1, D_MODEL), lambda i, f: (0, 0)),
        ],
        out_specs=pl.BlockSpec((TM_FFN, D_MODEL), lambda i, f: (i, 0)),
        out_shape=jax.ShapeDtypeStruct((t, D_MODEL), F32),
        scratch_shapes=[
            pltpu.VMEM((TM_FFN + 2 * CONV_HALO, D_MODEL), BF16),
            pltpu.VMEM((TM_FFN + 2 * CONV_HALO, TF_FFN), F32),
        ],
        compiler_params=pltpu.CompilerParams(
            dimension_semantics=("parallel", "arbitrary"), vmem_limit_bytes=VMEM_LIMIT_BYTES),
        name="ffn",
    )(z2, z2, z2, h2, w_up, w_up, cw, cb, w_down, gn)


def _rotary_tables(seq):
    inv_freq = ROPE_THETA ** (-jnp.arange(0, ROT_DIM, 2, dtype=F32) / ROT_DIM)
    ang = jnp.arange(seq).astype(F32)[:, None] * inv_freq[None, :]
    cos, sin = jnp.cos(ang), jnp.sin(ang)
    ones = jnp.ones((seq, HEAD_DIM - ROT_DIM), F32)
    zeros_h = jnp.zeros((seq, ROT_HALF), F32)
    zeros_r = jnp.zeros((seq, HEAD_DIM - ROT_DIM), F32)
    cos_t = jnp.concatenate([cos, cos, ones], axis=-1)
    sina_t = jnp.concatenate([-sin, zeros_h, zeros_r], axis=-1)
    sinb_t = jnp.concatenate([zeros_h, sin, zeros_r], axis=-1)
    return cos_t, sina_t, sinb_t


def kernel(x, norm_mix_g, w_in, w_attn_out, pool_w, pool_scale, w_pool_out, w_gate, w_out,
           norm_ffn_g, w_up, conv_w, conv_b, w_down, norm_final_g):
    b, seq, d = x.shape
    assert d == D_MODEL and seq % TM_IN == 0 and seq % TM_FFN == 0 and seq % TM_MIX == 0
    assert all(seq % (dil * ATTN_QB) == 0 for _, dil in DILATION_PAIRS)
    t = b * seq
    x2 = x.reshape(t, d)

    cos_t, sina_t, sinb_t = _rotary_tables(seq)
    proj = _inproj(x2, norm_mix_g.reshape(1, d), w_gate.astype(BF16), w_in.astype(BF16),
                   cos_t, sina_t, sinb_t, seq)
    attn = _attention(proj.reshape(b, seq, PROJ_COLS))
    h, z = _mix(attn.reshape(t, ATTN_OUT_WIDTH), proj, x2,
                w_attn_out.astype(BF16), pool_w.astype(BF16), pool_scale.reshape(1, POOL_WIDTH),
                w_pool_out.astype(BF16), w_out.astype(BF16), norm_ffn_g.reshape(1, d), seq)
    out = _ffn(z, h, w_up.astype(BF16), conv_w, conv_b.reshape(1, D_FF), w_down.astype(BF16),
               norm_final_g.reshape(1, d), seq)
    return out.reshape(b, seq, d)
```

```python
import functools

import numpy as np
import jax
import jax.numpy as jnp
from jax import lax
from jax.experimental import pallas as pl
from jax.experimental.pallas import tpu as pltpu

F32 = jnp.float32
BF16 = jnp.bfloat16

D_MODEL = 2048
HEAD_DIM = 128
DILATION_PAIRS = ((128, 1), (512, 4), (2048, 16))
N_GROUPS = len(DILATION_PAIRS)
HEADS_PER_GROUP = 4
N_ATTN_HEADS = N_GROUPS * HEADS_PER_GROUP
ATTN_WIDTH = N_ATTN_HEADS * HEAD_DIM
ATTN_OUT_WIDTH = HEADS_PER_GROUP * HEAD_DIM
ROT_DIM = HEAD_DIM // 4
ROT_HALF = ROT_DIM // 2
ROPE_THETA = 500000.0
POOL_WINDOWS = (2, 4, 8, 16)
POOL_WIDTH = 512
POOL_GROUP_WIDTH = 128
D_FF = 3 * D_MODEL
NORM_EPS = 1e-6
NEG_INF = -1e30
RADIUS = 64
assert all(w // 2 // d == RADIUS for w, d in DILATION_PAIRS)

LANES = 128
BF16_SUBLANES = 16
VMEM_LIMIT_BYTES = 56 * 1024 * 1024

GATE_COLS = 2 * D_MODEL
Q_COL = GATE_COLS
K_COL = Q_COL + ATTN_WIDTH
V_COL = K_COL + ATTN_WIDTH
P_COL = V_COL + ATTN_WIDTH
PROJ_COLS = P_COL + POOL_WIDTH

TM_IN, TN_IN = 1024, 1024
ROT_TILE_LO = GATE_COLS // TN_IN
ROT_TILE_HI = V_COL // TN_IN
assert GATE_COLS % TN_IN == 0 and V_COL % TN_IN == 0 and PROJ_COLS % TN_IN == 0
ATTN_QB = 128
ATTN_KW = ATTN_QB + 2 * RADIUS
ATTN_UNROLL = 16
ATTN_PERM_ROWS = 256
TM_MIX = 512
POOL_HALO = BF16_SUBLANES
TM_FFN, TF_FFN = 512, 1024
CONV_HALO = BF16_SUBLANES


def _rmsnorm(x, g):
    return x * lax.rsqrt(jnp.mean(x * x, axis=-1, keepdims=True) + NORM_EPS) * g


def _inproj_body(x_ref, g_ref, wg_ref, wi_ref, cos_ref, sina_ref, sinb_ref, o_ref, u_ref):
    j = pl.program_id(1)

    @pl.when(j == 0)
    def _():
        u_ref[...] = _rmsnorm(x_ref[...], g_ref[...]).astype(BF16)

    @pl.when(j < ROT_TILE_LO)
    def _():
        acc = jnp.dot(u_ref[...], wg_ref[...], preferred_element_type=F32)
        o_ref[...] = jax.nn.sigmoid(acc).astype(BF16)

    @pl.when((j >= ROT_TILE_LO) & (j < ROT_TILE_HI))
    def _():
        acc = jnp.dot(u_ref[...], wi_ref[...], preferred_element_type=F32)
        c = cos_ref[...]
        sa = sina_ref[...]
        sb = sinb_ref[...]
        for hd in range(TN_IN // HEAD_DIM):
            sl = slice(hd * HEAD_DIM, (hd + 1) * HEAD_DIM)
            blk = acc[:, sl]
            up = pltpu.roll(blk, HEAD_DIM - ROT_HALF, axis=1)
            dn = pltpu.roll(blk, ROT_HALF, axis=1)
            o_ref[:, sl] = (blk * c + up * sa + dn * sb).astype(BF16)

    @pl.when(j >= ROT_TILE_HI)
    def _():
        acc = jnp.dot(u_ref[...], wi_ref[...], preferred_element_type=F32)
        o_ref[...] = acc.astype(BF16)


def _inproj(x2, g, w_gate, w_in, cos_t, sina_t, sinb_t, seq):
    t = x2.shape[0]
    pos_tiles = seq // TM_IN
    gate_tiles = GATE_COLS // TN_IN
    table_spec = pl.BlockSpec((TM_IN, HEAD_DIM), lambda i, j: (i % pos_tiles, 0))
    return pl.pallas_call(
        _inproj_body,
        grid=(t // TM_IN, PROJ_COLS // TN_IN),
        in_specs=[
            pl.BlockSpec((TM_IN, D_MODEL), lambda i, j: (i, 0)),
            pl.BlockSpec((1, D_MODEL), lambda i, j: (0, 0)),
            pl.BlockSpec((D_MODEL, TN_IN), lambda i, j: (0, jnp.minimum(j, gate_tiles - 1))),
            pl.BlockSpec((D_MODEL, TN_IN), lambda i, j: (0, jnp.maximum(j - gate_tiles, 0))),
            table_spec, table_spec, table_spec,
        ],
        out_specs=pl.BlockSpec((TM_IN, TN_IN), lambda i, j: (i, j)),
        out_shape=jax.ShapeDtypeStruct((t, PROJ_COLS), BF16),
        scratch_shapes=[pltpu.VMEM((TM_IN, D_MODEL), BF16)],
        compiler_params=pltpu.CompilerParams(
            dimension_semantics=("parallel", "arbitrary"), vmem_limit_bytes=VMEM_LIMIT_BYTES),
        name="inproj",
    )(x2, g, w_gate, w_in, cos_t, sina_t, sinb_t)


def _perm_matrices():
    mats = []
    for _, d in DILATION_PAIRS:
        seg = ATTN_PERM_ROWS // d
        p = np.zeros((ATTN_PERM_ROWS, ATTN_PERM_ROWS), np.float32)
        for r in range(d):
            for l in range(seg):
                p[r * seg + l, l * d + r] = 1.0
        mats.append(p)
    return jnp.asarray(np.stack(mats), dtype=BF16)


def _attn_group(gi, d, seq, q_ref, k_ref, v_ref, perm_ref, qd, kd, vd, mask, bias, og, lg):
    L = seq // d
    LP = L + 2 * RADIUS
    nqb = L // ATTN_QB
    seg = ATTN_PERM_ROWS // d

    def deinterleave(src_ref, dst_ref, padded):
        for c in range(seq // ATTN_PERM_ROWS):
            x = src_ref[c * ATTN_PERM_ROWS: (c + 1) * ATTN_PERM_ROWS, :]
            y = jnp.dot(perm_ref[gi], x, preferred_element_type=F32).astype(BF16)
            for r in range(d):
                base = (r * LP + RADIUS if padded else r * L) + c * seg
                dst_ref[base: base + seg, :] = y[r * seg: (r + 1) * seg, :]

    zero_pad = jnp.zeros((RADIUS, HEAD_DIM), BF16)
    for r in range(d):
        for dst in (kd, vd):
            dst[r * LP: r * LP + RADIUS, :] = zero_pad
            dst[r * LP + RADIUS + L: (r + 1) * LP, :] = zero_pad
    if d == 1:
        q_src = q_ref
        kd[RADIUS: RADIUS + seq, :] = k_ref[...]
        vd[RADIUS: RADIUS + seq, :] = v_ref[...]
    else:
        q_src = qd
        deinterleave(q_ref, qd, False)
        deinterleave(k_ref, kd, True)
        deinterleave(v_ref, vd, True)

    scale = HEAD_DIM ** -0.5

    def body(blk, carry):
        r = blk // nqb
        qb = blk % nqb
        q0 = pl.multiple_of(r * L + qb * ATTN_QB, ATTN_QB)
        k0 = pl.multiple_of(r * LP + qb * ATTN_QB, ATTN_QB)
        q = q_src[pl.ds(q0, ATTN_QB), :]
        kw = kd[pl.ds(k0, ATTN_KW), :]
        vw = vd[pl.ds(k0, ATTN_KW), :]
        kind = jnp.where(qb == 0, 1, jnp.where(qb == nqb - 1, 2, 0))
        s = lax.dot_general(q, kw, (((1,), (1,)), ((), ())), preferred_element_type=F32)
        s = s * scale + bias[kind]
        m = jnp.max(s, axis=-1, keepdims=True)
        p = jnp.exp(s - m) * mask[kind]
        den = jnp.sum(p, axis=-1, keepdims=True)
        pv = jnp.dot(p.astype(BF16), vw, preferred_element_type=F32)
        o = pv / den
        lse = jnp.broadcast_to(m + jnp.log(den), (ATTN_QB, HEAD_DIM))
        if d == 1:
            t0 = pl.multiple_of(qb * ATTN_QB, ATTN_QB)
            og[gi][pl.ds(t0, ATTN_QB), :] = o
            lg[gi][pl.ds(t0, ATTN_QB), :] = lse
        else:
            t0 = qb * (ATTN_QB * d) + r
            og[gi][pl.ds(t0, ATTN_QB, stride=d), :] = o
            lg[gi][pl.ds(t0, ATTN_QB, stride=d), :] = lse
        return carry

    lax.fori_loop(0, seq // ATTN_QB, body, 0, unroll=ATTN_UNROLL)


def _attn_body(seq, q_ref, k_ref, v_ref, perm_ref, wu_ref, wd_ref, o_ref, wub_ref, wdb_ref,
               qd, kd, vd, mask, bias, og0, og1, og2, lg0, lg1, lg2):
    g = pl.program_id(2)
    og = (og0, og1, og2)
    lg = (lg0, lg1, lg2)

    wub_ref[...] = wu_ref[...].astype(BF16)
    wdb_ref[...] = wd_ref[...].astype(BF16)

    row = lax.broadcasted_iota(jnp.int32, (ATTN_QB, ATTN_KW), 0)
    col = lax.broadcasted_iota(jnp.int32, (ATTN_QB, ATTN_KW), 1)
    band = (col >= row) & (col <= row + 2 * RADIUS)
    for kind, valid in enumerate((band, band & (col >= RADIUS), band & (col < ATTN_QB + RADIUS))):
        mask[kind] = jnp.where(valid, 1.0, 0.0)
        bias[kind] = jnp.where(valid, 0.0, NEG_INF)

    for gi, (_, d) in enumerate(DILATION_PAIRS):
        @pl.when(g == gi)
        def _(gi=gi, d=d):
            _attn_group(gi, d, seq, q_ref, k_ref, v_ref, perm_ref, qd, kd, vd, mask, bias, og, lg)

    @pl.when(g == N_GROUPS - 1)
    def _():
        for c in range(seq // 256):
            sl = slice(c * 256, (c + 1) * 256)
            l0, l1, l2 = lg0[sl, :], lg1[sl, :], lg2[sl, :]
            mx = jnp.maximum(jnp.maximum(l0, l1), l2)
            e0, e1, e2 = jnp.exp(l0 - mx), jnp.exp(l1 - mx), jnp.exp(l2 - mx)
            num = e0 * og0[sl, :] + e1 * og1[sl, :] + e2 * og2[sl, :]
            o_ref[sl, :] = (num / (e0 + e1 + e2)).astype(BF16)


def _attention(proj3, w_up, w_down):
    b, seq, _ = proj3.shape
    max_d = max(d for _, d in DILATION_PAIRS)
    steps = b * HEADS_PER_GROUP * N_GROUPS
    up_cols = w_up.shape[1] // steps
    down_rows = w_down.shape[0] // steps
    assert w_up.shape[1] % (steps * LANES) == 0 and w_down.shape[0] % (steps * BF16_SUBLANES) == 0

    def head_spec(col0):
        blk0 = col0 // HEAD_DIM
        return pl.BlockSpec((None, seq, HEAD_DIM),
                            lambda bi, hs, g: (bi, 0, blk0 + g * HEADS_PER_GROUP + hs))

    def step(bi, hs, g):
        return (bi * HEADS_PER_GROUP + hs) * N_GROUPS + g

    up_spec = pl.BlockSpec((w_up.shape[0], up_cols), lambda bi, hs, g: (0, step(bi, hs, g)))
    down_spec = pl.BlockSpec((down_rows, w_down.shape[1]), lambda bi, hs, g: (step(bi, hs, g), 0))
    tok = pltpu.VMEM((seq, HEAD_DIM), F32)
    win = pltpu.VMEM((3, ATTN_QB, ATTN_KW), F32)
    return pl.pallas_call(
        functools.partial(_attn_body, seq),
        grid=(b, HEADS_PER_GROUP, N_GROUPS),
        in_specs=[head_spec(Q_COL), head_spec(K_COL), head_spec(V_COL),
                  pl.BlockSpec((N_GROUPS, ATTN_PERM_ROWS, ATTN_PERM_ROWS),
                               lambda bi, hs, g: (0, 0, 0)),
                  up_spec, down_spec],
        out_specs=[pl.BlockSpec((None, seq, HEAD_DIM), lambda bi, hs, g: (bi, 0, hs)),
                   up_spec, down_spec],
        out_shape=[jax.ShapeDtypeStruct((b, seq, ATTN_OUT_WIDTH), BF16),
                   jax.ShapeDtypeStruct(w_up.shape, BF16),
                   jax.ShapeDtypeStruct(w_down.shape, BF16)],
        scratch_shapes=[
            pltpu.VMEM((seq, HEAD_DIM), BF16),
            pltpu.VMEM((seq + 2 * RADIUS * max_d, HEAD_DIM), BF16),
            pltpu.VMEM((seq + 2 * RADIUS * max_d, HEAD_DIM), BF16),
            win, win,
            tok, tok, tok, tok, tok, tok,
        ],
        compiler_params=pltpu.CompilerParams(
            dimension_semantics=("parallel", "parallel", "arbitrary"),
            vmem_limit_bytes=VMEM_LIMIT_BYTES),
        name="attn",
    )(proj3, proj3, proj3, _perm_matrices(), w_up, w_down)


def _mix_body(seq, attn_ref, p_ref, pprev_ref, pnext_ref, ga_ref, gb_ref, x_ref,
              wa_ref, pw_ref, ps_ref, wp_ref, wo_ref, gn_ref, h_ref, z_ref, pad, pm):
    i = pl.program_id(0)
    tiles_per_seq = seq // TM_MIX
    ti = i % tiles_per_seq
    pad[0:POOL_HALO, :] = jnp.where(ti == 0, 0.0, pprev_ref[...].astype(F32))
    pad[POOL_HALO: POOL_HALO + TM_MIX, :] = p_ref[...].astype(F32)
    pad[POOL_HALO + TM_MIX:, :] = jnp.where(ti == tiles_per_seq - 1, 0.0, pnext_ref[...].astype(F32))

    t = ti * TM_MIX + lax.broadcasted_iota(jnp.int32, (TM_MIX, POOL_GROUP_WIDTH), 0)
    for g, w in enumerate(POOL_WINDOWS):
        cs = slice(g * POOL_GROUP_WIDTH, (g + 1) * POOL_GROUP_WIDTH)
        tot = pad[POOL_HALO - w // 2: POOL_HALO - w // 2 + TM_MIX, cs]
        for j in range(-w // 2 + 1, w // 2):
            tot = tot + pad[POOL_HALO + j: POOL_HALO + j + TM_MIX, cs]
        cnt = (jnp.minimum(t + w // 2, seq) - jnp.maximum(t - w // 2, 0)).astype(F32)
        pooled = tot / cnt - pad[POOL_HALO: POOL_HALO + TM_MIX, cs]
        pmg = jnp.dot(pooled.astype(BF16), pw_ref[g], preferred_element_type=F32) * ps_ref[:, cs]
        pm[:, cs] = pmg.astype(BF16)

    y_a = jnp.dot(attn_ref[...], wa_ref[...], preferred_element_type=F32)
    y_b = jnp.dot(pm[...], wp_ref[...], preferred_element_type=F32)
    mixed = ga_ref[...].astype(F32) * y_a + gb_ref[...].astype(F32) * y_b
    h = x_ref[...] + jnp.dot(mixed.astype(BF16), wo_ref[...], preferred_element_type=F32)
    h_ref[...] = h
    z_ref[...] = _rmsnorm(h, gn_ref[...]).astype(BF16)


def _mix(attn2, proj2, x2, wa, pw, ps, wp, wo, gn, seq):
    t = x2.shape[0]
    halo_blocks = TM_MIX // POOL_HALO
    n_halo = t // POOL_HALO
    p_blk = P_COL // POOL_WIDTH
    const = dict(pipeline_mode=pl.Buffered(1))
    return pl.pallas_call(
        functools.partial(_mix_body, seq),
        grid=(t // TM_MIX,),
        in_specs=[
            pl.BlockSpec((TM_MIX, ATTN_OUT_WIDTH), lambda i: (i, 0)),
            pl.BlockSpec((TM_MIX, POOL_WIDTH), lambda i: (i, p_blk)),
            pl.BlockSpec((POOL_HALO, POOL_WIDTH),
                         lambda i: (jnp.maximum(i * halo_blocks - 1, 0), p_blk)),
            pl.BlockSpec((POOL_HALO, POOL_WIDTH),
                         lambda i: (jnp.minimum((i + 1) * halo_blocks, n_halo - 1), p_blk)),
            pl.BlockSpec((TM_MIX, D_MODEL), lambda i: (i, 0)),
            pl.BlockSpec((TM_MIX, D_MODEL), lambda i: (i, 1)),
            pl.BlockSpec((TM_MIX, D_MODEL), lambda i: (i, 0)),
            pl.BlockSpec((ATTN_OUT_WIDTH, D_MODEL), lambda i: (0, 0), **const),
            pl.BlockSpec((len(POOL_WINDOWS), POOL_GROUP_WIDTH, POOL_GROUP_WIDTH),
                         lambda i: (0, 0, 0), **const),
            pl.BlockSpec((1, POOL_WIDTH), lambda i: (0, 0)),
            pl.BlockSpec((POOL_WIDTH, D_MODEL), lambda i: (0, 0), **const),
            pl.BlockSpec((D_MODEL, D_MODEL), lambda i: (0, 0), **const),
            pl.BlockSpec((1, D_MODEL), lambda i: (0, 0)),
        ],
        out_specs=[
            pl.BlockSpec((TM_MIX, D_MODEL), lambda i: (i, 0)),
            pl.BlockSpec((TM_MIX, D_MODEL), lambda i: (i, 0)),
        ],
        out_shape=[
            jax.ShapeDtypeStruct((t, D_MODEL), F32),
            jax.ShapeDtypeStruct((t, D_MODEL), BF16),
        ],
        scratch_shapes=[
            pltpu.VMEM((TM_MIX + 2 * POOL_HALO, POOL_WIDTH), F32),
            pltpu.VMEM((TM_MIX, POOL_WIDTH), BF16),
        ],
        compiler_params=pltpu.CompilerParams(
            dimension_semantics=("parallel",), vmem_limit_bytes=VMEM_LIMIT_BYTES),
        name="mix",
    )(attn2, proj2, proj2, proj2, proj2, proj2, x2, wa, pw, ps, wp, wo, gn)


def _ffn_body(seq, z_ref, zprev_ref, znext_ref, h_ref, wa_ref, wb_ref, cw_ref, cb_ref, wd_ref,
              gn_ref, o_ref, zext, aext):
    i = pl.program_id(0)
    f = pl.program_id(1)
    tiles_per_seq = seq // TM_FFN
    ti = i % tiles_per_seq

    @pl.when(f == 0)
    def _():
        zero_rows = jnp.zeros((CONV_HALO, D_MODEL), BF16)
        zext[0:CONV_HALO, :] = jnp.where(ti == 0, zero_rows, zprev_ref[...])
        zext[CONV_HALO: CONV_HALO + TM_FFN, :] = z_ref[...]
        zext[CONV_HALO + TM_FFN:, :] = jnp.where(ti == tiles_per_seq - 1, zero_rows, znext_ref[...])
        o_ref[...] = h_ref[...]

    aext[...] = jnp.dot(zext[...], wa_ref[...], preferred_element_type=F32)
    b = jnp.dot(zext[CONV_HALO: CONV_HALO + TM_FFN, :], wb_ref[...], preferred_element_type=F32)
    cw = cw_ref[...]
    a = (aext[CONV_HALO - 1: CONV_HALO - 1 + TM_FFN, :] * cw[0:1, :]
         + aext[CONV_HALO: CONV_HALO + TM_FFN, :] * cw[1:2, :]
         + aext[CONV_HALO + 1: CONV_HALO + 1 + TM_FFN, :] * cw[2:3, :]
         + cb_ref[...])
    gelu = 0.5 * a * (1.0 + lax.erf(a * np.float32(np.sqrt(0.5))))
    o_ref[...] += jnp.dot((gelu * b).astype(BF16), wd_ref[...], preferred_element_type=F32)

    @pl.when(f == pl.num_programs(1) - 1)
    def _():
        o_ref[...] = _rmsnorm(o_ref[...], gn_ref[...])


def _ffn(z2, h2, w_up, cw, cb, w_down, gn, seq):
    t = z2.shape[0]
    halo_blocks = TM_FFN // CONV_HALO
    n_halo = t // CONV_HALO
    nf = D_FF // TF_FFN
    return pl.pallas_call(
        functools.partial(_ffn_body, seq),
        grid=(t // TM_FFN, nf),
        in_specs=[
            pl.BlockSpec((TM_FFN, D_MODEL), lambda i, f: (i, 0)),
            pl.BlockSpec((CONV_HALO, D_MODEL), lambda i, f: (jnp.maximum(i * halo_blocks - 1, 0), 0)),
            pl.BlockSpec((CONV_HALO, D_MODEL),
                         lambda i, f: (jnp.minimum((i + 1) * halo_blocks, n_halo - 1), 0)),
            pl.BlockSpec((TM_FFN, D_MODEL), lambda i, f: (i, 0)),
            pl.BlockSpec((D_MODEL, TF_FFN), lambda i, f: (0, f)),
            pl.BlockSpec((D_MODEL, TF_FFN), lambda i, f: (0, nf + f)),
            pl.BlockSpec((3, TF_FFN), lambda i, f: (0, f)),
            pl.BlockSpec((1, TF_FFN), lambda i, f: (0, f)),
            pl.BlockSpec((TF_FFN, D_MODEL), lambda i, f: (f, 0)),
            pl.BlockSpec((1, D_MODEL), lambda i, f: (0, 0)),
        ],
        out_specs=pl.BlockSpec((TM_FFN, D_MODEL), lambda i, f: (i, 0)),
        out_shape=jax.ShapeDtypeStruct((t, D_MODEL), F32),
        scratch_shapes=[
            pltpu.VMEM((TM_FFN + 2 * CONV_HALO, D_MODEL), BF16),
            pltpu.VMEM((TM_FFN + 2 * CONV_HALO, TF_FFN), F32),
        ],
        compiler_params=pltpu.CompilerParams(
            dimension_semantics=("parallel", "arbitrary"), vmem_limit_bytes=VMEM_LIMIT_BYTES),
        name="ffn",
    )(z2, z2, z2, h2, w_up, w_up, cw, cb, w_down, gn)


def _rotary_tables(seq):
    inv_freq = ROPE_THETA ** (-jnp.arange(0, ROT_DIM, 2, dtype=F32) / ROT_DIM)
    ang = jnp.arange(seq).astype(F32)[:, None] * inv_freq[None, :]
    cos, sin = jnp.cos(ang), jnp.sin(ang)
    ones = jnp.ones((seq, HEAD_DIM - ROT_DIM), F32)
    zeros_h = jnp.zeros((seq, ROT_HALF), F32)
    zeros_r = jnp.zeros((seq, HEAD_DIM - ROT_DIM), F32)
    cos_t = jnp.concatenate([cos, cos, ones], axis=-1)
    sina_t = jnp.concatenate([-sin, zeros_h, zeros_r], axis=-1)
    sinb_t = jnp.concatenate([zeros_h, sin, zeros_r], axis=-1)
    return cos_t, sina_t, sinb_t


def kernel(x, norm_mix_g, w_in, w_attn_out, pool_w, pool_scale, w_pool_out, w_gate, w_out,
           norm_ffn_g, w_up, conv_w, conv_b, w_down, norm_final_g):
    b, seq, d = x.shape
    assert d == D_MODEL and seq % TM_IN == 0 and seq % TM_FFN == 0 and seq % TM_MIX == 0
    assert all(seq % (dil * ATTN_QB) == 0 for _, dil in DILATION_PAIRS)
    t = b * seq
    x2 = x.reshape(t, d)

    cos_t, sina_t, sinb_t = _rotary_tables(seq)
    proj = _inproj(x2, norm_mix_g.reshape(1, d), w_gate.astype(BF16), w_in.astype(BF16),
                   cos_t, sina_t, sinb_t, seq)
    attn, w_up_b, w_down_b = _attention(proj.reshape(b, seq, PROJ_COLS), w_up, w_down)
    h, z = _mix(attn.reshape(t, ATTN_OUT_WIDTH), proj, x2,
                w_attn_out.astype(BF16), pool_w.astype(BF16), pool_scale.reshape(1, POOL_WIDTH),
                w_pool_out.astype(BF16), w_out.astype(BF16), norm_ffn_g.reshape(1, d), seq)
    out = _ffn(z, h, w_up_b, conv_w, conv_b.reshape(1, D_FF), w_down_b,
               norm_final_g.reshape(1, d), seq)
    return out.reshape(b, seq, d)
```

```python
import functools

import numpy as np
import jax
import jax.numpy as jnp
from jax import lax
from jax.experimental import pallas as pl
from jax.experimental.pallas import tpu as pltpu

F32 = jnp.float32
BF16 = jnp.bfloat16

D_MODEL = 2048
HEAD_DIM = 128
DILATION_PAIRS = ((128, 1), (512, 4), (2048, 16))
N_GROUPS = len(DILATION_PAIRS)
HEADS_PER_GROUP = 4
N_ATTN_HEADS = N_GROUPS * HEADS_PER_GROUP
ATTN_WIDTH = N_ATTN_HEADS * HEAD_DIM
ATTN_OUT_WIDTH = HEADS_PER_GROUP * HEAD_DIM
ROT_DIM = HEAD_DIM // 4
ROT_HALF = ROT_DIM // 2
ROPE_THETA = 500000.0
POOL_WINDOWS = (2, 4, 8, 16)
POOL_WIDTH = 512
POOL_GROUP_WIDTH = 128
D_FF = 3 * D_MODEL
NORM_EPS = 1e-6
NEG_INF = -1e30
RADIUS = 64
assert all(w // 2 // d == RADIUS for w, d in DILATION_PAIRS)

LANES = 128
BF16_SUBLANES = 16
MXU_COLS = 256
VMEM_LIMIT_BYTES = 56 * 1024 * 1024

GATE_COLS = 2 * D_MODEL
Q_COL = GATE_COLS
K_COL = Q_COL + ATTN_WIDTH
V_COL = K_COL + ATTN_WIDTH
P_COL = V_COL + ATTN_WIDTH
PROJ_COLS = P_COL + POOL_WIDTH

TM_IN, TN_IN = 1024, 1024
IN_ROW_CHUNKS = 4
ROT_TILE_LO = GATE_COLS // TN_IN
ROT_TILE_HI = V_COL // TN_IN
assert GATE_COLS % TN_IN == 0 and V_COL % TN_IN == 0 and PROJ_COLS % TN_IN == 0
ATTN_QB = 128
ATTN_KW = ATTN_QB + 2 * RADIUS
ATTN_UNROLL = 32
ATTN_PERM_ROWS = 256
TM_MIX = 512
POOL_HALO = BF16_SUBLANES
TM_FFN, TF_FFN = 512, 1024
CONV_HALO = BF16_SUBLANES


def _rmsnorm(x, g):
    return x * lax.rsqrt(jnp.mean(x * x, axis=-1, keepdims=True) + NORM_EPS) * g


def _inproj_body(x_ref, g_ref, wg_ref, wi_ref, cos_ref, sina_ref, sinb_ref, o_ref, u_ref):
    j = pl.program_id(1)

    @pl.when(j == 0)
    def _():
        u_ref[...] = _rmsnorm(x_ref[...], g_ref[...]).astype(BF16)

    def chunked(w_ref, epilogue):
        for c in range(IN_ROW_CHUNKS):
            rows = slice(c * (TM_IN // IN_ROW_CHUNKS), (c + 1) * (TM_IN // IN_ROW_CHUNKS))
            acc = jnp.dot(u_ref[rows, :], w_ref[...], preferred_element_type=F32)
            epilogue(acc, rows)

    def sigmoid(acc, rows):
        o_ref[rows, :] = jax.nn.sigmoid(acc).astype(BF16)

    def rotary(acc, rows):
        c = cos_ref[rows, :]
        sa = sina_ref[rows, :]
        sb = sinb_ref[rows, :]
        for hd in range(TN_IN // HEAD_DIM):
            sl = slice(hd * HEAD_DIM, (hd + 1) * HEAD_DIM)
            blk = acc[:, sl]
            up = pltpu.roll(blk, HEAD_DIM - ROT_HALF, axis=1)
            dn = pltpu.roll(blk, ROT_HALF, axis=1)
            o_ref[rows, sl] = (blk * c + up * sa + dn * sb).astype(BF16)

    def plain(acc, rows):
        o_ref[rows, :] = acc.astype(BF16)

    @pl.when(j < ROT_TILE_LO)
    def _():
        chunked(wg_ref, sigmoid)

    @pl.when((j >= ROT_TILE_LO) & (j < ROT_TILE_HI))
    def _():
        chunked(wi_ref, rotary)

    @pl.when(j >= ROT_TILE_HI)
    def _():
        chunked(wi_ref, plain)


def _inproj(x2, g, w_gate, w_in, cos_t, sina_t, sinb_t, seq):
    t = x2.shape[0]
    pos_tiles = seq // TM_IN
    gate_tiles = GATE_COLS // TN_IN
    table_spec = pl.BlockSpec((TM_IN, HEAD_DIM), lambda i, j: (i % pos_tiles, 0))
    return pl.pallas_call(
        _inproj_body,
        grid=(t // TM_IN, PROJ_COLS // TN_IN),
        in_specs=[
            pl.BlockSpec((TM_IN, D_MODEL), lambda i, j: (i, 0)),
            pl.BlockSpec((1, D_MODEL), lambda i, j: (0, 0)),
            pl.BlockSpec((D_MODEL, TN_IN), lambda i, j: (0, jnp.minimum(j, gate_tiles - 1))),
            pl.BlockSpec((D_MODEL, TN_IN), lambda i, j: (0, jnp.maximum(j - gate_tiles, 0))),
            table_spec, table_spec, table_spec,
        ],
        out_specs=pl.BlockSpec((TM_IN, TN_IN), lambda i, j: (i, j)),
        out_shape=jax.ShapeDtypeStruct((t, PROJ_COLS), BF16),
        scratch_shapes=[pltpu.VMEM((TM_IN, D_MODEL), BF16)],
        compiler_params=pltpu.CompilerParams(
            dimension_semantics=("parallel", "arbitrary"), vmem_limit_bytes=VMEM_LIMIT_BYTES),
        name="inproj",
    )(x2, g, w_gate, w_in, cos_t, sina_t, sinb_t)


def _perm_matrices():
    mats = []
    for _, d in DILATION_PAIRS:
        seg = ATTN_PERM_ROWS // d
        p = np.zeros((ATTN_PERM_ROWS, ATTN_PERM_ROWS), np.float32)
        for r in range(d):
            for l in range(seg):
                p[r * seg + l, l * d + r] = 1.0
        mats.append(p)
    return jnp.asarray(np.stack(mats), dtype=BF16)


def _attn_group(gi, d, seq, q_ref, k_ref, v_ref, perm_ref, qd, kd, vd, mask, bias, og, lg):
    L = seq // d
    LP = L + 2 * RADIUS
    nqb = L // ATTN_QB
    seg = ATTN_PERM_ROWS // d

    def deinterleave(src_ref, dst_ref, padded):
        for c in range(seq // ATTN_PERM_ROWS):
            x = src_ref[c * ATTN_PERM_ROWS: (c + 1) * ATTN_PERM_ROWS, :]
            y = jnp.dot(perm_ref[gi], x, preferred_element_type=F32).astype(BF16)
            for r in range(d):
                base = (r * LP + RADIUS if padded else r * L) + c * seg
                dst_ref[base: base + seg, :] = y[r * seg: (r + 1) * seg, :]

    zero_pad = jnp.zeros((RADIUS, HEAD_DIM), BF16)
    for r in range(d):
        for dst in (kd, vd):
            dst[r * LP: r * LP + RADIUS, :] = zero_pad
            dst[r * LP + RADIUS + L: (r + 1) * LP, :] = zero_pad
    if d == 1:
        q_src = q_ref
        kd[RADIUS: RADIUS + seq, :] = k_ref[...]
        vd[RADIUS: RADIUS + seq, :] = v_ref[...]
    else:
        q_src = qd
        deinterleave(q_ref, qd, False)
        deinterleave(k_ref, kd, True)
        deinterleave(v_ref, vd, True)

    scale = HEAD_DIM ** -0.5

    def body(blk, carry):
        r = blk // nqb
        qb = blk % nqb
        q0 = pl.multiple_of(r * L + qb * ATTN_QB, ATTN_QB)
        k0 = pl.multiple_of(r * LP + qb * ATTN_QB, ATTN_QB)
        q = q_src[pl.ds(q0, ATTN_QB), :]
        kw = kd[pl.ds(k0, ATTN_KW), :]
        vw = vd[pl.ds(k0, ATTN_KW), :]
        kind = jnp.where(qb == 0, 1, jnp.where(qb == nqb - 1, 2, 0))
        s = lax.dot_general(q, kw, (((1,), (1,)), ((), ())), preferred_element_type=F32)
        s = s * scale + bias[kind]
        m = jnp.max(s, axis=-1, keepdims=True)
        p = jnp.exp(s - m) * mask[kind]
        den = jnp.sum(p, axis=-1, keepdims=True)
        pv = jnp.dot(p.astype(BF16), vw, preferred_element_type=F32)
        o = pv / den
        lse = jnp.broadcast_to(m + jnp.log(den), (ATTN_QB, HEAD_DIM))
        if d == 1:
            t0 = pl.multiple_of(qb * ATTN_QB, ATTN_QB)
            og[gi][pl.ds(t0, ATTN_QB), :] = o
            lg[gi][pl.ds(t0, ATTN_QB), :] = lse
        else:
            t0 = qb * (ATTN_QB * d) + r
            og[gi][pl.ds(t0, ATTN_QB, stride=d), :] = o
            lg[gi][pl.ds(t0, ATTN_QB, stride=d), :] = lse
        return carry

    lax.fori_loop(0, seq // ATTN_QB, body, 0, unroll=ATTN_UNROLL)


def _attn_body(seq, q_ref, k_ref, v_ref, perm_ref, wu_ref, wd_ref, o_ref, wub_ref, wdb_ref,
               qd, kd, vd, mask, bias, og0, og1, og2, lg0, lg1, lg2):
    g = pl.program_id(2)
    og = (og0, og1, og2)
    lg = (lg0, lg1, lg2)

    wub_ref[...] = wu_ref[...].astype(BF16)
    wdb_ref[...] = wd_ref[...].astype(BF16)

    row = lax.broadcasted_iota(jnp.int32, (ATTN_QB, ATTN_KW), 0)
    col = lax.broadcasted_iota(jnp.int32, (ATTN_QB, ATTN_KW), 1)
    band = (col >= row) & (col <= row + 2 * RADIUS)
    for kind, valid in enumerate((band, band & (col >= RADIUS), band & (col < ATTN_QB + RADIUS))):
        mask[kind] = jnp.where(valid, 1.0, 0.0)
        bias[kind] = jnp.where(valid, 0.0, NEG_INF)

    for gi, (_, d) in enumerate(DILATION_PAIRS):
        @pl.when(g == gi)
        def _(gi=gi, d=d):
            _attn_group(gi, d, seq, q_ref, k_ref, v_ref, perm_ref, qd, kd, vd, mask, bias, og, lg)

    @pl.when(g == N_GROUPS - 1)
    def _():
        for c in range(seq // 256):
            sl = slice(c * 256, (c + 1) * 256)
            l0, l1, l2 = lg0[sl, :], lg1[sl, :], lg2[sl, :]
            mx = jnp.maximum(jnp.maximum(l0, l1), l2)
            e0, e1, e2 = jnp.exp(l0 - mx), jnp.exp(l1 - mx), jnp.exp(l2 - mx)
            num = e0 * og0[sl, :] + e1 * og1[sl, :] + e2 * og2[sl, :]
            o_ref[sl, :] = (num / (e0 + e1 + e2)).astype(BF16)


def _attention(proj3, w_up, w_down):
    b, seq, _ = proj3.shape
    max_d = max(d for _, d in DILATION_PAIRS)
    steps = b * HEADS_PER_GROUP * N_GROUPS
    up_cols = w_up.shape[1] // steps
    down_rows = w_down.shape[0] // steps
    assert w_up.shape[1] % (steps * LANES) == 0 and w_down.shape[0] % (steps * BF16_SUBLANES) == 0

    def head_spec(col0):
        blk0 = col0 // HEAD_DIM
        return pl.BlockSpec((None, seq, HEAD_DIM),
                            lambda bi, hs, g: (bi, 0, blk0 + g * HEADS_PER_GROUP + hs))

    def step(bi, hs, g):
        return (bi * HEADS_PER_GROUP + hs) * N_GROUPS + g

    up_spec = pl.BlockSpec((w_up.shape[0], up_cols), lambda bi, hs, g: (0, step(bi, hs, g)))
    down_spec = pl.BlockSpec((down_rows, w_down.shape[1]), lambda bi, hs, g: (step(bi, hs, g), 0))
    tok = pltpu.VMEM((seq, HEAD_DIM), F32)
    win = pltpu.VMEM((3, ATTN_QB, ATTN_KW), F32)
    return pl.pallas_call(
        functools.partial(_attn_body, seq),
        grid=(b, HEADS_PER_GROUP, N_GROUPS),
        in_specs=[head_spec(Q_COL), head_spec(K_COL), head_spec(V_COL),
                  pl.BlockSpec((N_GROUPS, ATTN_PERM_ROWS, ATTN_PERM_ROWS),
                               lambda bi, hs, g: (0, 0, 0)),
                  up_spec, down_spec],
        out_specs=[pl.BlockSpec((None, seq, HEAD_DIM), lambda bi, hs, g: (bi, 0, hs)),
                   up_spec, down_spec],
        out_shape=[jax.ShapeDtypeStruct((b, seq, ATTN_OUT_WIDTH), BF16),
                   jax.ShapeDtypeStruct(w_up.shape, BF16),
                   jax.ShapeDtypeStruct(w_down.shape, BF16)],
        scratch_shapes=[
            pltpu.VMEM((seq, HEAD_DIM), BF16),
            pltpu.VMEM((seq + 2 * RADIUS * max_d, HEAD_DIM), BF16),
            pltpu.VMEM((seq + 2 * RADIUS * max_d, HEAD_DIM), BF16),
            win, win,
            tok, tok, tok, tok, tok, tok,
        ],
        compiler_params=pltpu.CompilerParams(
            dimension_semantics=("parallel", "parallel", "arbitrary"),
            vmem_limit_bytes=VMEM_LIMIT_BYTES),
        name="attn",
    )(proj3, proj3, proj3, _perm_matrices(), w_up, w_down)


def _mix_body(seq, attn_ref, p_ref, pprev_ref, pnext_ref, ga_ref, gb_ref, x_ref,
              wa_ref, pw_ref, ps_ref, wp_ref, wo_ref, gn_ref, h_ref, z_ref, pad, pm):
    i = pl.program_id(0)
    tiles_per_seq = seq // TM_MIX
    ti = i % tiles_per_seq
    pad[0:POOL_HALO, :] = jnp.where(ti == 0, 0.0, pprev_ref[...].astype(F32))
    pad[POOL_HALO: POOL_HALO + TM_MIX, :] = p_ref[...].astype(F32)
    pad[POOL_HALO + TM_MIX:, :] = jnp.where(ti == tiles_per_seq - 1, 0.0, pnext_ref[...].astype(F32))

    t = ti * TM_MIX + lax.broadcasted_iota(jnp.int32, (TM_MIX, POOL_GROUP_WIDTH), 0)
    for g, w in enumerate(POOL_WINDOWS):
        cs = slice(g * POOL_GROUP_WIDTH, (g + 1) * POOL_GROUP_WIDTH)
        tot = pad[POOL_HALO - w // 2: POOL_HALO - w // 2 + TM_MIX, cs]
        for j in range(-w // 2 + 1, w // 2):
            tot = tot + pad[POOL_HALO + j: POOL_HALO + j + TM_MIX, cs]
        cnt = (jnp.minimum(t + w // 2, seq) - jnp.maximum(t - w // 2, 0)).astype(F32)
        pooled = tot / cnt - pad[POOL_HALO: POOL_HALO + TM_MIX, cs]
        pmg = jnp.dot(pooled.astype(BF16), pw_ref[g], preferred_element_type=F32) * ps_ref[:, cs]
        pm[:, cs] = pmg.astype(BF16)

    y_a = jnp.dot(attn_ref[...], wa_ref[...], preferred_element_type=F32)
    y_b = jnp.dot(pm[...], wp_ref[...], preferred_element_type=F32)
    mixed = ga_ref[...].astype(F32) * y_a + gb_ref[...].astype(F32) * y_b
    h = x_ref[...] + jnp.dot(mixed.astype(BF16), wo_ref[...], preferred_element_type=F32)
    h_ref[...] = h
    z_ref[...] = _rmsnorm(h, gn_ref[...]).astype(BF16)


def _mix(attn2, proj2, x2, wa, pw, ps, wp, wo, gn, seq):
    t = x2.shape[0]
    halo_blocks = TM_MIX // POOL_HALO
    n_halo = t // POOL_HALO
    p_blk = P_COL // POOL_WIDTH
    const = dict(pipeline_mode=pl.Buffered(1))
    return pl.pallas_call(
        functools.partial(_mix_body, seq),
        grid=(t // TM_MIX,),
        in_specs=[
            pl.BlockSpec((TM_MIX, ATTN_OUT_WIDTH), lambda i: (i, 0)),
            pl.BlockSpec((TM_MIX, POOL_WIDTH), lambda i: (i, p_blk)),
            pl.BlockSpec((POOL_HALO, POOL_WIDTH),
                         lambda i: (jnp.maximum(i * halo_blocks - 1, 0), p_blk)),
            pl.BlockSpec((POOL_HALO, POOL_WIDTH),
                         lambda i: (jnp.minimum((i + 1) * halo_blocks, n_halo - 1), p_blk)),
            pl.BlockSpec((TM_MIX, D_MODEL), lambda i: (i, 0)),
            pl.BlockSpec((TM_MIX, D_MODEL), lambda i: (i, 1)),
            pl.BlockSpec((TM_MIX, D_MODEL), lambda i: (i, 0)),
            pl.BlockSpec((ATTN_OUT_WIDTH, D_MODEL), lambda i: (0, 0), **const),
            pl.BlockSpec((len(POOL_WINDOWS), POOL_GROUP_WIDTH, POOL_GROUP_WIDTH),
                         lambda i: (0, 0, 0), **const),
            pl.BlockSpec((1, POOL_WIDTH), lambda i: (0, 0)),
            pl.BlockSpec((POOL_WIDTH, D_MODEL), lambda i: (0, 0), **const),
            pl.BlockSpec((D_MODEL, D_MODEL), lambda i: (0, 0), **const),
            pl.BlockSpec((1, D_MODEL), lambda i: (0, 0)),
        ],
        out_specs=[
            pl.BlockSpec((TM_MIX, D_MODEL), lambda i: (i, 0)),
            pl.BlockSpec((TM_MIX, D_MODEL), lambda i: (i, 0)),
        ],
        out_shape=[
            jax.ShapeDtypeStruct((t, D_MODEL), F32),
            jax.ShapeDtypeStruct((t, D_MODEL), BF16),
        ],
        scratch_shapes=[
            pltpu.VMEM((TM_MIX + 2 * POOL_HALO, POOL_WIDTH), F32),
            pltpu.VMEM((TM_MIX, POOL_WIDTH), BF16),
        ],
        compiler_params=pltpu.CompilerParams(
            dimension_semantics=("parallel",), vmem_limit_bytes=VMEM_LIMIT_BYTES),
        name="mix",
    )(attn2, proj2, proj2, proj2, proj2, proj2, x2, wa, pw, ps, wp, wo, gn)


def _ffn_body(seq, z_ref, zprev_ref, znext_ref, h_ref, wa_ref, wb_ref, cw_ref, cb_ref, wd_ref,
              gn_ref, o_ref, zext, aext):
    i = pl.program_id(0)
    f = pl.program_id(1)
    tiles_per_seq = seq // TM_FFN
    ti = i % tiles_per_seq

    @pl.when(f == 0)
    def _():
        zero_rows = jnp.zeros((CONV_HALO, D_MODEL), BF16)
        zext[0:CONV_HALO, :] = jnp.where(ti == 0, zero_rows, zprev_ref[...])
        zext[CONV_HALO: CONV_HALO + TM_FFN, :] = z_ref[...]
        zext[CONV_HALO + TM_FFN:, :] = jnp.where(ti == tiles_per_seq - 1, zero_rows, znext_ref[...])
        o_ref[...] = h_ref[...]

    aext[...] = jnp.dot(zext[...], wa_ref[...], preferred_element_type=F32)
    b = jnp.dot(zext[CONV_HALO: CONV_HALO + TM_FFN, :], wb_ref[...], preferred_element_type=F32)
    cw = cw_ref[...]
    a = (aext[CONV_HALO - 1: CONV_HALO - 1 + TM_FFN, :] * cw[0:1, :]
         + aext[CONV_HALO: CONV_HALO + TM_FFN, :] * cw[1:2, :]
         + aext[CONV_HALO + 1: CONV_HALO + 1 + TM_FFN, :] * cw[2:3, :]
         + cb_ref[...])
    gelu = 0.5 * a * (1.0 + lax.erf(a * np.float32(np.sqrt(0.5))))
    o_ref[...] += jnp.dot((gelu * b).astype(BF16), wd_ref[...], preferred_element_type=F32)

    @pl.when(f == pl.num_programs(1) - 1)
    def _():
        o_ref[...] = _rmsnorm(o_ref[...], gn_ref[...])


def _ffn(z2, h2, w_up, cw, cb, w_down, gn, seq):
    t = z2.shape[0]
    halo_blocks = TM_FFN // CONV_HALO
    n_halo = t // CONV_HALO
    nf = D_FF // TF_FFN
    return pl.pallas_call(
        functools.partial(_ffn_body, seq),
        grid=(t // TM_FFN, nf),
        in_specs=[
            pl.BlockSpec((TM_FFN, D_MODEL), lambda i, f: (i, 0)),
            pl.BlockSpec((CONV_HALO, D_MODEL), lambda i, f: (jnp.maximum(i * halo_blocks - 1, 0), 0)),
            pl.BlockSpec((CONV_HALO, D_MODEL),
                         lambda i, f: (jnp.minimum((i + 1) * halo_blocks, n_halo - 1), 0)),
            pl.BlockSpec((TM_FFN, D_MODEL), lambda i, f: (i, 0)),
            pl.BlockSpec((D_MODEL, TF_FFN), lambda i, f: (0, f)),
            pl.BlockSpec((D_MODEL, TF_FFN), lambda i, f: (0, nf + f)),
            pl.BlockSpec((3, TF_FFN), lambda i, f: (0, f)),
            pl.BlockSpec((1, TF_FFN), lambda i, f: (0, f)),
            pl.BlockSpec((TF_FFN, D_MODEL), lambda i, f: (f, 0)),
            pl.BlockSpec((1, D_MODEL), lambda i, f: (0, 0)),
        ],
        out_specs=pl.BlockSpec((TM_FFN, D_MODEL), lambda i, f: (i, 0)),
        out_shape=jax.ShapeDtypeStruct((t, D_MODEL), F32),
        scratch_shapes=[
            pltpu.VMEM((TM_FFN + 2 * CONV_HALO, D_MODEL), BF16),
            pltpu.VMEM((TM_FFN + 2 * CONV_HALO, TF_FFN), F32),
        ],
        compiler_params=pltpu.CompilerParams(
            dimension_semantics=("parallel", "arbitrary"), vmem_limit_bytes=VMEM_LIMIT_BYTES),
        name="ffn",
    )(z2, z2, z2, h2, w_up, w_up, cw, cb, w_down, gn)


def _rotary_tables(seq):
    inv_freq = ROPE_THETA ** (-np.arange(0, ROT_DIM, 2, dtype=np.float64) / ROT_DIM)
    ang = np.arange(seq, dtype=np.float64)[:, None] * inv_freq[None, :]
    cos, sin = np.cos(ang), np.sin(ang)
    ones = np.ones((seq, HEAD_DIM - ROT_DIM))
    zeros_h = np.zeros((seq, ROT_HALF))
    zeros_r = np.zeros((seq, HEAD_DIM - ROT_DIM))
    cos_t = np.concatenate([cos, cos, ones], axis=-1)
    sina_t = np.concatenate([-sin, zeros_h, zeros_r], axis=-1)
    sinb_t = np.concatenate([zeros_h, sin, zeros_r], axis=-1)
    return tuple(jnp.asarray(t, dtype=F32) for t in (cos_t, sina_t, sinb_t))


def kernel(x, norm_mix_g, w_in, w_attn_out, pool_w, pool_scale, w_pool_out, w_gate, w_out,
           norm_ffn_g, w_up, conv_w, conv_b, w_down, norm_final_g):
    b, seq, d = x.shape
    assert d == D_MODEL and seq % TM_IN == 0 and seq % TM_FFN == 0 and seq % TM_MIX == 0
    assert all(seq % (dil * ATTN_QB) == 0 for _, dil in DILATION_PAIRS)
    t = b * seq
    x2 = x.reshape(t, d)

    cos_t, sina_t, sinb_t = _rotary_tables(seq)
    proj = _inproj(x2, norm_mix_g.reshape(1, d), w_gate.astype(BF16), w_in.astype(BF16),
                   cos_t, sina_t, sinb_t, seq)
    attn, w_up_b, w_down_b = _attention(proj.reshape(b, seq, PROJ_COLS), w_up, w_down)
    h, z = _mix(attn.reshape(t, ATTN_OUT_WIDTH), proj, x2,
                w_attn_out.astype(BF16), pool_w.astype(BF16), pool_scale.reshape(1, POOL_WIDTH),
                w_pool_out.astype(BF16), w_out.astype(BF16), norm_ffn_g.reshape(1, d), seq)
    out = _ffn(z, h, w_up_b, conv_w, conv_b.reshape(1, D_FF), w_down_b,
               norm_final_g.reshape(1, d), seq)
    return out.reshape(b, seq, d)
```

```python
import functools

import numpy as np
import jax
import jax.numpy as jnp
from jax import lax
from jax.experimental import pallas as pl
from jax.experimental.pallas import tpu as pltpu

F32 = jnp.float32
BF16 = jnp.bfloat16

D_MODEL = 2048
HEAD_DIM = 128
DILATION_PAIRS = ((128, 1), (512, 4), (2048, 16))
N_GROUPS = len(DILATION_PAIRS)
HEADS_PER_GROUP = 4
N_ATTN_HEADS = N_GROUPS * HEADS_PER_GROUP
ATTN_WIDTH = N_ATTN_HEADS * HEAD_DIM
ATTN_OUT_WIDTH = HEADS_PER_GROUP * HEAD_DIM
ROT_DIM = HEAD_DIM // 4
ROT_HALF = ROT_DIM // 2
ROPE_THETA = 500000.0
POOL_WINDOWS = (2, 4, 8, 16)
POOL_WIDTH = 512
POOL_GROUP_WIDTH = 128
D_FF = 3 * D_MODEL
NORM_EPS = 1e-6
NEG_INF = -1e30
RADIUS = 64
assert all(w // 2 // d == RADIUS for w, d in DILATION_PAIRS)

LANES = 128
BF16_SUBLANES = 16
MXU_COLS = 256
VMEM_LIMIT_BYTES = 56 * 1024 * 1024

GATE_COLS = 2 * D_MODEL
Q_COL = GATE_COLS
K_COL = Q_COL + ATTN_WIDTH
V_COL = K_COL + ATTN_WIDTH
P_COL = V_COL + ATTN_WIDTH
PROJ_COLS = P_COL + POOL_WIDTH

TM_IN, TN_IN = 1024, 1024
IN_ROW_CHUNKS = 4
ROT_TILE_LO = GATE_COLS // TN_IN
ROT_TILE_HI = V_COL // TN_IN
assert GATE_COLS % TN_IN == 0 and V_COL % TN_IN == 0 and PROJ_COLS % TN_IN == 0
ATTN_QB = 128
ATTN_KW = ATTN_QB + 2 * RADIUS
ATTN_UNROLL = 32
ATTN_PERM_ROWS = 256
CAST_ROWS = 128
TM_MIX = 512
MIX_ROW_CHUNKS = 2
POOL_HALO = BF16_SUBLANES
TM_FFN, TF_FFN = 512, 1024
CONV_HALO = BF16_SUBLANES


def _rmsnorm(x, g):
    return x * lax.rsqrt(jnp.mean(x * x, axis=-1, keepdims=True) + NORM_EPS) * g


def _inproj_body(x_ref, g_ref, wg_ref, wi_ref, cos_ref, sina_ref, sinb_ref, o_ref, u_ref):
    j = pl.program_id(1)

    def chunked(w_ref, epilogue):
        for c in range(IN_ROW_CHUNKS):
            rows = slice(c * (TM_IN // IN_ROW_CHUNKS), (c + 1) * (TM_IN // IN_ROW_CHUNKS))
            acc = jnp.dot(u_ref[rows, :], w_ref[...], preferred_element_type=F32)
            epilogue(acc, rows)

    def sigmoid(acc, rows):
        o_ref[rows, :] = jax.nn.sigmoid(acc).astype(BF16)

    def rotary(acc, rows):
        c = cos_ref[rows, :]
        sa = sina_ref[rows, :]
        sb = sinb_ref[rows, :]
        for hd in range(TN_IN // HEAD_DIM):
            sl = slice(hd * HEAD_DIM, (hd + 1) * HEAD_DIM)
            blk = acc[:, sl]
            up = pltpu.roll(blk, HEAD_DIM - ROT_HALF, axis=1)
            dn = pltpu.roll(blk, ROT_HALF, axis=1)
            o_ref[rows, sl] = (blk * c + up * sa + dn * sb).astype(BF16)

    def plain(acc, rows):
        o_ref[rows, :] = acc.astype(BF16)

    @pl.when(j == 0)
    def _():
        for c in range(IN_ROW_CHUNKS):
            rows = slice(c * (TM_IN // IN_ROW_CHUNKS), (c + 1) * (TM_IN // IN_ROW_CHUNKS))
            u = _rmsnorm(x_ref[rows, :], g_ref[...]).astype(BF16)
            u_ref[rows, :] = u
            sigmoid(jnp.dot(u, wg_ref[...], preferred_element_type=F32), rows)

    @pl.when((j > 0) & (j < ROT_TILE_LO))
    def _():
        chunked(wg_ref, sigmoid)

    @pl.when((j >= ROT_TILE_LO) & (j < ROT_TILE_HI))
    def _():
        chunked(wi_ref, rotary)

    @pl.when(j >= ROT_TILE_HI)
    def _():
        chunked(wi_ref, plain)


def _inproj(x2, g, w_gate, w_in, cos_t, sina_t, sinb_t, seq):
    t = x2.shape[0]
    pos_tiles = seq // TM_IN
    gate_tiles = GATE_COLS // TN_IN
    table_spec = pl.BlockSpec((TM_IN, HEAD_DIM), lambda i, j: (i % pos_tiles, 0))
    return pl.pallas_call(
        _inproj_body,
        grid=(t // TM_IN, PROJ_COLS // TN_IN),
        in_specs=[
            pl.BlockSpec((TM_IN, D_MODEL), lambda i, j: (i, 0)),
            pl.BlockSpec((1, D_MODEL), lambda i, j: (0, 0)),
            pl.BlockSpec((D_MODEL, TN_IN), lambda i, j: (0, jnp.minimum(j, gate_tiles - 1))),
            pl.BlockSpec((D_MODEL, TN_IN), lambda i, j: (0, jnp.maximum(j - gate_tiles, 0))),
            table_spec, table_spec, table_spec,
        ],
        out_specs=pl.BlockSpec((TM_IN, TN_IN), lambda i, j: (i, j)),
        out_shape=jax.ShapeDtypeStruct((t, PROJ_COLS), BF16),
        scratch_shapes=[pltpu.VMEM((TM_IN, D_MODEL), BF16)],
        compiler_params=pltpu.CompilerParams(
            dimension_semantics=("parallel", "arbitrary"), vmem_limit_bytes=VMEM_LIMIT_BYTES),
        name="inproj",
    )(x2, g, w_gate, w_in, cos_t, sina_t, sinb_t)


def _perm_matrices():
    mats = []
    for _, d in DILATION_PAIRS:
        seg = ATTN_PERM_ROWS // d
        p = np.zeros((ATTN_PERM_ROWS, ATTN_PERM_ROWS), np.float32)
        for r in range(d):
            for l in range(seg):
                p[r * seg + l, l * d + r] = 1.0
        mats.append(p)
    return jnp.asarray(np.stack(mats), dtype=BF16)


def _attn_group(gi, d, seq, q_ref, k_ref, v_ref, perm_ref, qd, kd, vd, mask, bias, og, lg):
    L = seq // d
    LP = L + 2 * RADIUS
    nqb = L // ATTN_QB
    seg = ATTN_PERM_ROWS // d

    def deinterleave(src_ref, dst_ref, padded):
        for c in range(seq // ATTN_PERM_ROWS):
            x = src_ref[c * ATTN_PERM_ROWS: (c + 1) * ATTN_PERM_ROWS, :]
            y = jnp.dot(perm_ref[gi], x, preferred_element_type=F32).astype(BF16)
            for r in range(d):
                base = (r * LP + RADIUS if padded else r * L) + c * seg
                dst_ref[base: base + seg, :] = y[r * seg: (r + 1) * seg, :]

    zero_pad = jnp.zeros((RADIUS, HEAD_DIM), BF16)
    for r in range(d):
        for dst in (kd, vd):
            dst[r * LP: r * LP + RADIUS, :] = zero_pad
            dst[r * LP + RADIUS + L: (r + 1) * LP, :] = zero_pad
    if d == 1:
        q_src = q_ref
        kd[RADIUS: RADIUS + seq, :] = k_ref[...]
        vd[RADIUS: RADIUS + seq, :] = v_ref[...]
    else:
        q_src = qd
        deinterleave(q_ref, qd, False)
        deinterleave(k_ref, kd, True)
        deinterleave(v_ref, vd, True)

    scale = HEAD_DIM ** -0.5

    def body(blk, carry):
        r = blk // nqb
        qb = blk % nqb
        q0 = pl.multiple_of(r * L + qb * ATTN_QB, ATTN_QB)
        k0 = pl.multiple_of(r * LP + qb * ATTN_QB, ATTN_QB)
        q = q_src[pl.ds(q0, ATTN_QB), :]
        kw = kd[pl.ds(k0, ATTN_KW), :]
        vw = vd[pl.ds(k0, ATTN_KW), :]
        kind = jnp.where(qb == 0, 1, jnp.where(qb == nqb - 1, 2, 0))
        s = lax.dot_general(q, kw, (((1,), (1,)), ((), ())), preferred_element_type=F32)
        s = s * scale + bias[kind]
        m = jnp.max(s, axis=-1, keepdims=True)
        p = jnp.exp(s - m) * mask[kind]
        den = jnp.sum(p, axis=-1, keepdims=True)
        pv = jnp.dot(p.astype(BF16), vw, preferred_element_type=F32)
        o = pv / den
        lse = jnp.broadcast_to(m + jnp.log(den), (ATTN_QB, HEAD_DIM))
        if d == 1:
            t0 = pl.multiple_of(qb * ATTN_QB, ATTN_QB)
            og[gi][pl.ds(t0, ATTN_QB), :] = o
            lg[gi][pl.ds(t0, ATTN_QB), :] = lse
        else:
            t0 = qb * (ATTN_QB * d) + r
            og[gi][pl.ds(t0, ATTN_QB, stride=d), :] = o
            lg[gi][pl.ds(t0, ATTN_QB, stride=d), :] = lse
        return carry

    lax.fori_loop(0, seq // ATTN_QB, body, 0, unroll=ATTN_UNROLL)


def _attn_body(seq, n_cast, q_ref, k_ref, v_ref, perm_ref, *refs):
    cast_in, o_ref, cast_out = refs[:n_cast], refs[n_cast], refs[n_cast + 1: 2 * n_cast + 1]
    qd, kd, vd, mask, bias, og0, og1, og2, lg0, lg1, lg2 = refs[2 * n_cast + 1:]
    g = pl.program_id(2)
    og = (og0, og1, og2)
    lg = (lg0, lg1, lg2)

    for src, dst in zip(cast_in, cast_out):
        dst[...] = src[...].astype(BF16)

    row = lax.broadcasted_iota(jnp.int32, (ATTN_QB, ATTN_KW), 0)
    col = lax.broadcasted_iota(jnp.int32, (ATTN_QB, ATTN_KW), 1)
    band = (col >= row) & (col <= row + 2 * RADIUS)
    for kind, valid in enumerate((band, band & (col >= RADIUS), band & (col < ATTN_QB + RADIUS))):
        mask[kind] = jnp.where(valid, 1.0, 0.0)
        bias[kind] = jnp.where(valid, 0.0, NEG_INF)

    for gi, (_, d) in enumerate(DILATION_PAIRS):
        @pl.when(g == gi)
        def _(gi=gi, d=d):
            _attn_group(gi, d, seq, q_ref, k_ref, v_ref, perm_ref, qd, kd, vd, mask, bias, og, lg)

    @pl.when(g == N_GROUPS - 1)
    def _():
        for c in range(seq // 256):
            sl = slice(c * 256, (c + 1) * 256)
            l0, l1, l2 = lg0[sl, :], lg1[sl, :], lg2[sl, :]
            mx = jnp.maximum(jnp.maximum(l0, l1), l2)
            e0, e1, e2 = jnp.exp(l0 - mx), jnp.exp(l1 - mx), jnp.exp(l2 - mx)
            num = e0 * og0[sl, :] + e1 * og1[sl, :] + e2 * og2[sl, :]
            o_ref[sl, :] = (num / (e0 + e1 + e2)).astype(BF16)


def _attention(proj3, w_up, w_down, row_sliced):
    b, seq, _ = proj3.shape
    max_d = max(d for _, d in DILATION_PAIRS)
    steps = b * HEADS_PER_GROUP * N_GROUPS
    up_cols = w_up.shape[1] // steps
    down_rows = w_down.shape[0] // steps
    assert w_up.shape[1] % (steps * LANES) == 0 and w_down.shape[0] % (steps * BF16_SUBLANES) == 0

    def head_spec(col0):
        blk0 = col0 // HEAD_DIM
        return pl.BlockSpec((None, seq, HEAD_DIM),
                            lambda bi, hs, g: (bi, 0, blk0 + g * HEADS_PER_GROUP + hs))

    def step(bi, hs, g):
        return (bi * HEADS_PER_GROUP + hs) * N_GROUPS + g

    def rows_spec(w):
        n = w.shape[0] // CAST_ROWS
        assert w.shape[0] % CAST_ROWS == 0 and n <= steps
        return pl.BlockSpec((CAST_ROWS, w.shape[1]),
                            lambda bi, hs, g: (jnp.minimum(step(bi, hs, g), n - 1), 0))

    cast_specs = [
        pl.BlockSpec((w_up.shape[0], up_cols), lambda bi, hs, g: (0, step(bi, hs, g))),
        pl.BlockSpec((down_rows, w_down.shape[1]), lambda bi, hs, g: (step(bi, hs, g), 0)),
    ] + [rows_spec(w) for w in row_sliced]
    cast_args = [w_up, w_down] + list(row_sliced)
    tok = pltpu.VMEM((seq, HEAD_DIM), F32)
    win = pltpu.VMEM((3, ATTN_QB, ATTN_KW), F32)
    return pl.pallas_call(
        functools.partial(_attn_body, seq, len(cast_args)),
        grid=(b, HEADS_PER_GROUP, N_GROUPS),
        in_specs=[head_spec(Q_COL), head_spec(K_COL), head_spec(V_COL),
                  pl.BlockSpec((N_GROUPS, ATTN_PERM_ROWS, ATTN_PERM_ROWS),
                               lambda bi, hs, g: (0, 0, 0))] + cast_specs,
        out_specs=[pl.BlockSpec((None, seq, HEAD_DIM), lambda bi, hs, g: (bi, 0, hs))] + cast_specs,
        out_shape=[jax.ShapeDtypeStruct((b, seq, ATTN_OUT_WIDTH), BF16)]
        + [jax.ShapeDtypeStruct(w.shape, BF16) for w in cast_args],
        scratch_shapes=[
            pltpu.VMEM((seq, HEAD_DIM), BF16),
            pltpu.VMEM((seq + 2 * RADIUS * max_d, HEAD_DIM), BF16),
            pltpu.VMEM((seq + 2 * RADIUS * max_d, HEAD_DIM), BF16),
            win, win,
            tok, tok, tok, tok, tok, tok,
        ],
        compiler_params=pltpu.CompilerParams(
            dimension_semantics=("arbitrary", "arbitrary", "arbitrary"),
            vmem_limit_bytes=VMEM_LIMIT_BYTES),
        name="attn",
    )(proj3, proj3, proj3, _perm_matrices(), *cast_args)


def _mix_body(seq, attn_ref, p_ref, pprev_ref, pnext_ref, ga_ref, gb_ref, x_ref,
              wa_ref, pw_ref, ps_ref, wp_ref, wo_ref, gn_ref, h_ref, z_ref, pad, pm):
    i = pl.program_id(0)
    tiles_per_seq = seq // TM_MIX
    ti = i % tiles_per_seq
    pad[0:POOL_HALO, :] = jnp.where(ti == 0, 0.0, pprev_ref[...].astype(F32))
    pad[POOL_HALO: POOL_HALO + TM_MIX, :] = p_ref[...].astype(F32)
    pad[POOL_HALO + TM_MIX:, :] = jnp.where(ti == tiles_per_seq - 1, 0.0, pnext_ref[...].astype(F32))

    ya_cols = D_MODEL // len(POOL_WINDOWS)
    ya_parts = []
    t = ti * TM_MIX + lax.broadcasted_iota(jnp.int32, (TM_MIX, POOL_GROUP_WIDTH), 0)
    for g, w in enumerate(POOL_WINDOWS):
        ya_parts.append(jnp.dot(attn_ref[...], wa_ref[:, g * ya_cols: (g + 1) * ya_cols],
                                preferred_element_type=F32))
        cs = slice(g * POOL_GROUP_WIDTH, (g + 1) * POOL_GROUP_WIDTH)
        tot = pad[POOL_HALO - w // 2: POOL_HALO - w // 2 + TM_MIX, cs]
        for j in range(-w // 2 + 1, w // 2):
            tot = tot + pad[POOL_HALO + j: POOL_HALO + j + TM_MIX, cs]
        cnt = (jnp.minimum(t + w // 2, seq) - jnp.maximum(t - w // 2, 0)).astype(F32)
        pooled = tot / cnt - pad[POOL_HALO: POOL_HALO + TM_MIX, cs]
        pmg = jnp.dot(pooled.astype(BF16), pw_ref[g], preferred_element_type=F32) * ps_ref[:, cs]
        pm[:, cs] = pmg.astype(BF16)

    y_a = jnp.concatenate(ya_parts, axis=1)
    y_b = jnp.dot(pm[...], wp_ref[...], preferred_element_type=F32)
    rc = TM_MIX // MIX_ROW_CHUNKS
    for c in range(MIX_ROW_CHUNKS):
        rows = slice(c * rc, (c + 1) * rc)
        mixed = ga_ref[rows, :].astype(F32) * y_a[rows] + gb_ref[rows, :].astype(F32) * y_b[rows]
        h = x_ref[rows, :] + jnp.dot(mixed.astype(BF16), wo_ref[...], preferred_element_type=F32)
        h_ref[rows, :] = h
        z_ref[rows, :] = _rmsnorm(h, gn_ref[...]).astype(BF16)


def _mix(attn2, proj2, x2, wa, pw, ps, wp, wo, gn, seq):
    t = x2.shape[0]
    halo_blocks = TM_MIX // POOL_HALO
    n_halo = t // POOL_HALO
    p_blk = P_COL // POOL_WIDTH
    const = dict(pipeline_mode=pl.Buffered(1))
    return pl.pallas_call(
        functools.partial(_mix_body, seq),
        grid=(t // TM_MIX,),
        in_specs=[
            pl.BlockSpec((TM_MIX, ATTN_OUT_WIDTH), lambda i: (i, 0)),
            pl.BlockSpec((TM_MIX, POOL_WIDTH), lambda i: (i, p_blk)),
            pl.BlockSpec((POOL_HALO, POOL_WIDTH),
                         lambda i: (jnp.maximum(i * halo_blocks - 1, 0), p_blk)),
            pl.BlockSpec((POOL_HALO, POOL_WIDTH),
                         lambda i: (jnp.minimum((i + 1) * halo_blocks, n_halo - 1), p_blk)),
            pl.BlockSpec((TM_MIX, D_MODEL), lambda i: (i, 0)),
            pl.BlockSpec((TM_MIX, D_MODEL), lambda i: (i, 1)),
            pl.BlockSpec((TM_MIX, D_MODEL), lambda i: (i, 0)),
            pl.BlockSpec((ATTN_OUT_WIDTH, D_MODEL), lambda i: (0, 0), **const),
            pl.BlockSpec((len(POOL_WINDOWS), POOL_GROUP_WIDTH, POOL_GROUP_WIDTH),
                         lambda i: (0, 0, 0), **const),
            pl.BlockSpec((1, POOL_WIDTH), lambda i: (0, 0)),
            pl.BlockSpec((POOL_WIDTH, D_MODEL), lambda i: (0, 0), **const),
            pl.BlockSpec((D_MODEL, D_MODEL), lambda i: (0, 0), **const),
            pl.BlockSpec((1, D_MODEL), lambda i: (0, 0)),
        ],
        out_specs=[
            pl.BlockSpec((TM_MIX, D_MODEL), lambda i: (i, 0)),
            pl.BlockSpec((TM_MIX, D_MODEL), lambda i: (i, 0)),
        ],
        out_shape=[
            jax.ShapeDtypeStruct((t, D_MODEL), F32),
            jax.ShapeDtypeStruct((t, D_MODEL), BF16),
        ],
        scratch_shapes=[
            pltpu.VMEM((TM_MIX + 2 * POOL_HALO, POOL_WIDTH), F32),
            pltpu.VMEM((TM_MIX, POOL_WIDTH), BF16),
        ],
        compiler_params=pltpu.CompilerParams(
            dimension_semantics=("parallel",), vmem_limit_bytes=VMEM_LIMIT_BYTES),
        name="mix",
    )(attn2, proj2, proj2, proj2, proj2, proj2, x2, wa, pw, ps, wp, wo, gn)


def _ffn_body(seq, z_ref, zprev_ref, znext_ref, h_ref, wa_ref, wb_ref, cw_ref, cb_ref, wd_ref,
              gn_ref, o_ref, zext, aext):
    i = pl.program_id(0)
    f = pl.program_id(1)
    tiles_per_seq = seq // TM_FFN
    ti = i % tiles_per_seq

    @pl.when(f == 0)
    def _():
        zero_rows = jnp.zeros((CONV_HALO, D_MODEL), BF16)
        zext[0:CONV_HALO, :] = jnp.where(ti == 0, zero_rows, zprev_ref[...])
        zext[CONV_HALO: CONV_HALO + TM_FFN, :] = z_ref[...]
        zext[CONV_HALO + TM_FFN:, :] = jnp.where(ti == tiles_per_seq - 1, zero_rows, znext_ref[...])
        o_ref[...] = h_ref[...]

    aext[...] = jnp.dot(zext[...], wa_ref[...], preferred_element_type=F32)
    b = jnp.dot(zext[CONV_HALO: CONV_HALO + TM_FFN, :], wb_ref[...], preferred_element_type=F32)
    cw = cw_ref[...]
    a = (aext[CONV_HALO - 1: CONV_HALO - 1 + TM_FFN, :] * cw[0:1, :]
         + aext[CONV_HALO: CONV_HALO + TM_FFN, :] * cw[1:2, :]
         + aext[CONV_HALO + 1: CONV_HALO + 1 + TM_FFN, :] * cw[2:3, :]
         + cb_ref[...])
    gelu = 0.5 * a * (1.0 + lax.erf(a * np.float32(np.sqrt(0.5))))
    o_ref[...] += jnp.dot((gelu * b).astype(BF16), wd_ref[...], preferred_element_type=F32)

    @pl.when(f == pl.num_programs(1) - 1)
    def _():
        o_ref[...] = _rmsnorm(o_ref[...], gn_ref[...])


def _ffn(z2, h2, w_up, cw, cb, w_down, gn, seq):
    t = z2.shape[0]
    halo_blocks = TM_FFN // CONV_HALO
    n_halo = t // CONV_HALO
    nf = D_FF // TF_FFN
    return pl.pallas_call(
        functools.partial(_ffn_body, seq),
        grid=(t // TM_FFN, nf),
        in_specs=[
            pl.BlockSpec((TM_FFN, D_MODEL), lambda i, f: (i, 0)),
            pl.BlockSpec((CONV_HALO, D_MODEL), lambda i, f: (jnp.maximum(i * halo_blocks - 1, 0), 0)),
            pl.BlockSpec((CONV_HALO, D_MODEL),
                         lambda i, f: (jnp.minimum((i + 1) * halo_blocks, n_halo - 1), 0)),
            pl.BlockSpec((TM_FFN, D_MODEL), lambda i, f: (i, 0)),
            pl.BlockSpec((D_MODEL, TF_FFN), lambda i, f: (0, f)),
            pl.BlockSpec((D_MODEL, TF_FFN), lambda i, f: (0, nf + f)),
            pl.BlockSpec((3, TF_FFN), lambda i, f: (0, f)),
            pl.BlockSpec((1, TF_FFN), lambda i, f: (0, f)),
            pl.BlockSpec((TF_FFN, D_MODEL), lambda i, f: (f, 0)),
            pl.BlockSpec((1, D_MODEL), lambda i, f: (0, 0)),
        ],
        out_specs=pl.BlockSpec((TM_FFN, D_MODEL), lambda i, f: (i, 0)),
        out_shape=jax.ShapeDtypeStruct((t, D_MODEL), F32),
        scratch_shapes=[
            pltpu.VMEM((TM_FFN + 2 * CONV_HALO, D_MODEL), BF16),
            pltpu.VMEM((TM_FFN + 2 * CONV_HALO, TF_FFN), F32),
        ],
        compiler_params=pltpu.CompilerParams(
            dimension_semantics=("parallel", "arbitrary"), vmem_limit_bytes=VMEM_LIMIT_BYTES),
        name="ffn",
    )(z2, z2, z2, h2, w_up, w_up, cw, cb, w_down, gn)


def _rotary_tables(seq):
    inv_freq = ROPE_THETA ** (-np.arange(0, ROT_DIM, 2, dtype=np.float64) / ROT_DIM)
    ang = np.arange(seq, dtype=np.float64)[:, None] * inv_freq[None, :]
    cos, sin = np.cos(ang), np.sin(ang)
    ones = np.ones((seq, HEAD_DIM - ROT_DIM))
    zeros_h = np.zeros((seq, ROT_HALF))
    zeros_r = np.zeros((seq, HEAD_DIM - ROT_DIM))
    cos_t = np.concatenate([cos, cos, ones], axis=-1)
    sina_t = np.concatenate([-sin, zeros_h, zeros_r], axis=-1)
    sinb_t = np.concatenate([zeros_h, sin, zeros_r], axis=-1)
    return tuple(jnp.asarray(t, dtype=F32) for t in (cos_t, sina_t, sinb_t))


def kernel(x, norm_mix_g, w_in, w_attn_out, pool_w, pool_scale, w_pool_out, w_gate, w_out,
           norm_ffn_g, w_up, conv_w, conv_b, w_down, norm_final_g):
    b, seq, d = x.shape
    assert d == D_MODEL and seq % TM_IN == 0 and seq % TM_FFN == 0 and seq % TM_MIX == 0
    assert all(seq % (dil * ATTN_QB) == 0 for _, dil in DILATION_PAIRS)
    t = b * seq
    x2 = x.reshape(t, d)

    cos_t, sina_t, sinb_t = _rotary_tables(seq)
    proj = _inproj(x2, norm_mix_g.reshape(1, d), w_gate.astype(BF16), w_in.astype(BF16),
                   cos_t, sina_t, sinb_t, seq)
    attn, w_up_b, w_down_b, w_attn_out_b, w_pool_out_b, w_out_b = _attention(
        proj.reshape(b, seq, PROJ_COLS), w_up, w_down, (w_attn_out, w_pool_out, w_out))
    h, z = _mix(attn.reshape(t, ATTN_OUT_WIDTH), proj, x2,
                w_attn_out_b, pool_w.astype(BF16), pool_scale.reshape(1, POOL_WIDTH),
                w_pool_out_b, w_out_b, norm_ffn_g.reshape(1, d), seq)
    out = _ffn(z, h, w_up_b, conv_w, conv_b.reshape(1, D_FF), w_down_b,
               norm_final_g.reshape(1, d), seq)
    return out.reshape(b, seq, d)
```

```python
import functools

import numpy as np
import jax
import jax.numpy as jnp
from jax import lax
from jax.experimental import pallas as pl
from jax.experimental.pallas import tpu as pltpu

F32 = jnp.float32
BF16 = jnp.bfloat16

D_MODEL = 2048
HEAD_DIM = 128
DILATION_PAIRS = ((128, 1), (512, 4), (2048, 16))
N_GROUPS = len(DILATION_PAIRS)
HEADS_PER_GROUP = 4
N_ATTN_HEADS = N_GROUPS * HEADS_PER_GROUP
ATTN_WIDTH = N_ATTN_HEADS * HEAD_DIM
ATTN_OUT_WIDTH = HEADS_PER_GROUP * HEAD_DIM
ROT_DIM = HEAD_DIM // 4
ROT_HALF = ROT_DIM // 2
ROPE_THETA = 500000.0
POOL_WINDOWS = (2, 4, 8, 16)
POOL_WIDTH = 512
POOL_GROUP_WIDTH = 128
D_FF = 3 * D_MODEL
NORM_EPS = 1e-6
NEG_INF = -1e30
RADIUS = 64
assert all(w // 2 // d == RADIUS for w, d in DILATION_PAIRS)

LANES = 128
BF16_SUBLANES = 16
MXU_COLS = 256
VMEM_LIMIT_BYTES = 56 * 1024 * 1024

GATE_COLS = 2 * D_MODEL
Q_COL = GATE_COLS
K_COL = Q_COL + ATTN_WIDTH
V_COL = K_COL + ATTN_WIDTH
P_COL = V_COL + ATTN_WIDTH
PROJ_COLS = P_COL + POOL_WIDTH

TM_IN, TN_IN = 1024, 1024
IN_ROW_CHUNKS = 4
ROT_TILE_LO = GATE_COLS // TN_IN
ROT_TILE_HI = V_COL // TN_IN
assert GATE_COLS % TN_IN == 0 and V_COL % TN_IN == 0 and PROJ_COLS % TN_IN == 0
ATTN_QB = 128
ATTN_KW = ATTN_QB + 2 * RADIUS
ATTN_UNROLL = 32
ATTN_PERM_ROWS = 256
ATTN_MERGE_ROWS = 256
CAST_ROWS = 128
TM_MIX = 512
MIX_ROW_CHUNKS = 2
POOL_HALO = BF16_SUBLANES
TM_FFN, TF_FFN = 512, 1024
CONV_HALO = BF16_SUBLANES


def _rmsnorm(x, g):
    return x * lax.rsqrt(jnp.mean(x * x, axis=-1, keepdims=True) + NORM_EPS) * g


def _inproj_body(x_ref, g_ref, wg_ref, wi_ref, cos_ref, sina_ref, sinb_ref, o_ref, u_ref):
    j = pl.program_id(1)

    def chunked(w_ref, epilogue):
        for c in range(IN_ROW_CHUNKS):
            rows = slice(c * (TM_IN // IN_ROW_CHUNKS), (c + 1) * (TM_IN // IN_ROW_CHUNKS))
            acc = jnp.dot(u_ref[rows, :], w_ref[...], preferred_element_type=F32)
            epilogue(acc, rows)

    def sigmoid(acc, rows):
        o_ref[rows, :] = jax.nn.sigmoid(acc).astype(BF16)

    def rotary(acc, rows):
        c = cos_ref[rows, :]
        sa = sina_ref[rows, :]
        sb = sinb_ref[rows, :]
        for hd in range(TN_IN // HEAD_DIM):
            sl = slice(hd * HEAD_DIM, (hd + 1) * HEAD_DIM)
            blk = acc[:, sl]
            up = pltpu.roll(blk, HEAD_DIM - ROT_HALF, axis=1)
            dn = pltpu.roll(blk, ROT_HALF, axis=1)
            o_ref[rows, sl] = (blk * c + up * sa + dn * sb).astype(BF16)

    def plain(acc, rows):
        o_ref[rows, :] = acc.astype(BF16)

    @pl.when(j == 0)
    def _():
        for c in range(IN_ROW_CHUNKS):
            rows = slice(c * (TM_IN // IN_ROW_CHUNKS), (c + 1) * (TM_IN // IN_ROW_CHUNKS))
            u = _rmsnorm(x_ref[rows, :], g_ref[...]).astype(BF16)
            u_ref[rows, :] = u
            sigmoid(jnp.dot(u, wg_ref[...], preferred_element_type=F32), rows)

    @pl.when((j > 0) & (j < ROT_TILE_LO))
    def _():
        chunked(wg_ref, sigmoid)

    @pl.when((j >= ROT_TILE_LO) & (j < ROT_TILE_HI))
    def _():
        chunked(wi_ref, rotary)

    @pl.when(j >= ROT_TILE_HI)
    def _():
        chunked(wi_ref, plain)


def _inproj(x2, g, w_gate, w_in, cos_t, sina_t, sinb_t, seq):
    t = x2.shape[0]
    pos_tiles = seq // TM_IN
    gate_tiles = GATE_COLS // TN_IN
    table_spec = pl.BlockSpec((TM_IN, HEAD_DIM), lambda i, j: (i % pos_tiles, 0))
    return pl.pallas_call(
        _inproj_body,
        grid=(t // TM_IN, PROJ_COLS // TN_IN),
        in_specs=[
            pl.BlockSpec((TM_IN, D_MODEL), lambda i, j: (i, 0)),
            pl.BlockSpec((1, D_MODEL), lambda i, j: (0, 0)),
            pl.BlockSpec((D_MODEL, TN_IN), lambda i, j: (0, jnp.minimum(j, gate_tiles - 1))),
            pl.BlockSpec((D_MODEL, TN_IN), lambda i, j: (0, jnp.maximum(j - gate_tiles, 0))),
            table_spec, table_spec, table_spec,
        ],
        out_specs=pl.BlockSpec((TM_IN, TN_IN), lambda i, j: (i, j)),
        out_shape=jax.ShapeDtypeStruct((t, PROJ_COLS), BF16),
        scratch_shapes=[pltpu.VMEM((TM_IN, D_MODEL), BF16)],
        compiler_params=pltpu.CompilerParams(
            dimension_semantics=("parallel", "arbitrary"), vmem_limit_bytes=VMEM_LIMIT_BYTES),
        name="inproj",
    )(x2, g, w_gate, w_in, cos_t, sina_t, sinb_t)


def _perm_matrices():
    mats = []
    for _, d in DILATION_PAIRS:
        seg = ATTN_PERM_ROWS // d
        p = np.zeros((ATTN_PERM_ROWS, ATTN_PERM_ROWS), np.float32)
        for r in range(d):
            for l in range(seg):
                p[r * seg + l, l * d + r] = 1.0
        mats.append(p)
    return jnp.asarray(np.stack(mats), dtype=BF16)


def _attn_group(gi, d, seq, q_ref, k_ref, v_ref, perm_ref, qd, kd, vd, mask, bias, og, lg):
    L = seq // d
    LP = L + 2 * RADIUS
    nqb = L // ATTN_QB
    seg = ATTN_PERM_ROWS // d

    def deinterleave(src_ref, dst_ref, padded):
        for c in range(seq // ATTN_PERM_ROWS):
            x = src_ref[c * ATTN_PERM_ROWS: (c + 1) * ATTN_PERM_ROWS, :]
            y = jnp.dot(perm_ref[gi], x, preferred_element_type=F32).astype(BF16)
            for r in range(d):
                base = (r * LP + RADIUS if padded else r * L) + c * seg
                dst_ref[base: base + seg, 0:HEAD_DIM] = y[r * seg: (r + 1) * seg, :]

    zero_pad = jnp.zeros((RADIUS, HEAD_DIM), BF16)
    for r in range(d):
        for dst in (kd, vd):
            dst[r * LP: r * LP + RADIUS, 0:HEAD_DIM] = zero_pad
            dst[r * LP + RADIUS + L: (r + 1) * LP, 0:HEAD_DIM] = zero_pad
    vd[0: d * LP, HEAD_DIM:] = jnp.ones((d * LP, HEAD_DIM), BF16)
    if d == 1:
        q_src = q_ref
        kd[RADIUS: RADIUS + seq, :] = k_ref[...]
        vd[RADIUS: RADIUS + seq, 0:HEAD_DIM] = v_ref[...]
    else:
        q_src = qd
        deinterleave(q_ref, qd, False)
        deinterleave(k_ref, kd, True)
        deinterleave(v_ref, vd, True)

    scale = HEAD_DIM ** -0.5

    def body(blk, carry):
        r = blk // nqb
        qb = blk % nqb
        q0 = pl.multiple_of(r * L + qb * ATTN_QB, ATTN_QB)
        k0 = pl.multiple_of(r * LP + qb * ATTN_QB, ATTN_QB)
        q = q_src[pl.ds(q0, ATTN_QB), :]
        kw = kd[pl.ds(k0, ATTN_KW), :]
        vw = vd[pl.ds(k0, ATTN_KW), :]
        kind = jnp.where(qb == 0, 1, jnp.where(qb == nqb - 1, 2, 0))
        s = lax.dot_general(q, kw, (((1,), (1,)), ((), ())), preferred_element_type=F32)
        s = s * scale + bias[kind]
        m = jnp.max(s, axis=-1, keepdims=True)
        p = jnp.exp(s - m) * mask[kind]
        pv = jnp.dot(p.astype(BF16), vw, preferred_element_type=F32)
        den = pv[:, HEAD_DIM:]
        o = pv[:, :HEAD_DIM] / den
        lse = m + jnp.log(den)
        if d == 1:
            t0 = pl.multiple_of(qb * ATTN_QB, ATTN_QB)
            og[gi][pl.ds(t0, ATTN_QB), :] = o
            lg[gi][pl.ds(t0, ATTN_QB), :] = lse
        else:
            t0 = qb * (ATTN_QB * d) + r
            og[gi][pl.ds(t0, ATTN_QB, stride=d), :] = o
            lg[gi][pl.ds(t0, ATTN_QB, stride=d), :] = lse
        return carry

    lax.fori_loop(0, seq // ATTN_QB, body, 0, unroll=ATTN_UNROLL)


def _attn_body(seq, n_cast, q_ref, k_ref, v_ref, perm_ref, *refs):
    cast_in, o_ref, cast_out = refs[:n_cast], refs[n_cast], refs[n_cast + 1: 2 * n_cast + 1]
    qd, kd, vd, mask, bias, og0, og1, og2, lg0, lg1, lg2 = refs[2 * n_cast + 1:]
    g = pl.program_id(2)
    og = (og0, og1, og2)
    lg = (lg0, lg1, lg2)

    for src, dst in zip(cast_in, cast_out):
        dst[...] = src[...].astype(BF16)

    row = lax.broadcasted_iota(jnp.int32, (ATTN_QB, ATTN_KW), 0)
    col = lax.broadcasted_iota(jnp.int32, (ATTN_QB, ATTN_KW), 1)
    band = (col >= row) & (col <= row + 2 * RADIUS)
    for kind, valid in enumerate((band, band & (col >= RADIUS), band & (col < ATTN_QB + RADIUS))):
        mask[kind] = jnp.where(valid, 1.0, 0.0)
        bias[kind] = jnp.where(valid, 0.0, NEG_INF)

    for gi, (_, d) in enumerate(DILATION_PAIRS):
        @pl.when(g == gi)
        def _(gi=gi, d=d):
            _attn_group(gi, d, seq, q_ref, k_ref, v_ref, perm_ref, qd, kd, vd, mask, bias, og, lg)

    @pl.when(g == N_GROUPS - 1)
    def _():
        for c in range(seq // ATTN_MERGE_ROWS):
            sl = slice(c * ATTN_MERGE_ROWS, (c + 1) * ATTN_MERGE_ROWS)
            l0, l1, l2 = lg0[sl, :], lg1[sl, :], lg2[sl, :]
            mx = jnp.maximum(jnp.maximum(l0, l1), l2)
            e0, e1, e2 = jnp.exp(l0 - mx), jnp.exp(l1 - mx), jnp.exp(l2 - mx)
            num = e0 * og0[sl, :] + e1 * og1[sl, :] + e2 * og2[sl, :]
            o_ref[sl, :] = (num / (e0 + e1 + e2)).astype(BF16)


def _attention(proj3, w_up, w_down, row_sliced):
    b, seq, _ = proj3.shape
    max_d = max(d for _, d in DILATION_PAIRS)
    steps = b * HEADS_PER_GROUP * N_GROUPS
    up_cols = w_up.shape[1] // steps
    down_rows = w_down.shape[0] // steps
    assert w_up.shape[1] % (steps * LANES) == 0 and w_down.shape[0] % (steps * BF16_SUBLANES) == 0

    def head_spec(col0):
        blk0 = col0 // HEAD_DIM
        return pl.BlockSpec((None, seq, HEAD_DIM),
                            lambda bi, hs, g: (bi, 0, blk0 + g * HEADS_PER_GROUP + hs))

    def step(bi, hs, g):
        return (bi * HEADS_PER_GROUP + hs) * N_GROUPS + g

    def rows_spec(w):
        n = w.shape[0] // CAST_ROWS
        assert w.shape[0] % CAST_ROWS == 0 and n <= steps
        return pl.BlockSpec((CAST_ROWS, w.shape[1]),
                            lambda bi, hs, g: (jnp.minimum(step(bi, hs, g), n - 1), 0))

    cast_specs = [
        pl.BlockSpec((w_up.shape[0], up_cols), lambda bi, hs, g: (0, step(bi, hs, g))),
        pl.BlockSpec((down_rows, w_down.shape[1]), lambda bi, hs, g: (step(bi, hs, g), 0)),
    ] + [rows_spec(w) for w in row_sliced]
    cast_args = [w_up, w_down] + list(row_sliced)
    tok = pltpu.VMEM((seq, HEAD_DIM), F32)
    win = pltpu.VMEM((3, ATTN_QB, ATTN_KW), F32)
    return pl.pallas_call(
        functools.partial(_attn_body, seq, len(cast_args)),
        grid=(b, HEADS_PER_GROUP, N_GROUPS),
        in_specs=[head_spec(Q_COL), head_spec(K_COL), head_spec(V_COL),
                  pl.BlockSpec((N_GROUPS, ATTN_PERM_ROWS, ATTN_PERM_ROWS),
                               lambda bi, hs, g: (0, 0, 0))] + cast_specs,
        out_specs=[pl.BlockSpec((None, seq, HEAD_DIM), lambda bi, hs, g: (bi, 0, hs))] + cast_specs,
        out_shape=[jax.ShapeDtypeStruct((b, seq, ATTN_OUT_WIDTH), BF16)]
        + [jax.ShapeDtypeStruct(w.shape, BF16) for w in cast_args],
        scratch_shapes=[
            pltpu.VMEM((seq, HEAD_DIM), BF16),
            pltpu.VMEM((seq + 2 * RADIUS * max_d, HEAD_DIM), BF16),
            pltpu.VMEM((seq + 2 * RADIUS * max_d, 2 * HEAD_DIM), BF16),
            win, win,
            tok, tok, tok, tok, tok, tok,
        ],
        compiler_params=pltpu.CompilerParams(
            dimension_semantics=("arbitrary", "arbitrary", "arbitrary"),
            vmem_limit_bytes=VMEM_LIMIT_BYTES),
        name="attn",
    )(proj3, proj3, proj3, _perm_matrices(), *cast_args)


def _mix_body(seq, attn_ref, p_ref, pprev_ref, pnext_ref, ga_ref, gb_ref, x_ref,
              wa_ref, pw_ref, ps_ref, wp_ref, wo_ref, gn_ref, h_ref, z_ref, pad, pm):
    i = pl.program_id(0)
    tiles_per_seq = seq // TM_MIX
    ti = i % tiles_per_seq
    pad[0:POOL_HALO, :] = jnp.where(ti == 0, 0.0, pprev_ref[...].astype(F32))
    pad[POOL_HALO: POOL_HALO + TM_MIX, :] = p_ref[...].astype(F32)
    pad[POOL_HALO + TM_MIX:, :] = jnp.where(ti == tiles_per_seq - 1, 0.0, pnext_ref[...].astype(F32))

    ya_cols = D_MODEL // len(POOL_WINDOWS)
    ya_parts = []
    t = ti * TM_MIX + lax.broadcasted_iota(jnp.int32, (TM_MIX, POOL_GROUP_WIDTH), 0)
    for g, w in enumerate(POOL_WINDOWS):
        ya_parts.append(jnp.dot(attn_ref[...], wa_ref[:, g * ya_cols: (g + 1) * ya_cols],
                                preferred_element_type=F32))
        cs = slice(g * POOL_GROUP_WIDTH, (g + 1) * POOL_GROUP_WIDTH)
        tot = pad[POOL_HALO - w // 2: POOL_HALO - w // 2 + TM_MIX, cs]
        for j in range(-w // 2 + 1, w // 2):
            tot = tot + pad[POOL_HALO + j: POOL_HALO + j + TM_MIX, cs]
        cnt = (jnp.minimum(t + w // 2, seq) - jnp.maximum(t - w // 2, 0)).astype(F32)
        pooled = tot / cnt - pad[POOL_HALO: POOL_HALO + TM_MIX, cs]
        pmg = jnp.dot(pooled.astype(BF16), pw_ref[g], preferred_element_type=F32) * ps_ref[:, cs]
        pm[:, cs] = pmg.astype(BF16)

    y_a = jnp.concatenate(ya_parts, axis=1)
    y_b = jnp.dot(pm[...], wp_ref[...], preferred_element_type=F32)
    rc = TM_MIX // MIX_ROW_CHUNKS
    for c in range(MIX_ROW_CHUNKS):
        rows = slice(c * rc, (c + 1) * rc)
        mixed = ga_ref[rows, :].astype(F32) * y_a[rows] + gb_ref[rows, :].astype(F32) * y_b[rows]
        h = x_ref[rows, :] + jnp.dot(mixed.astype(BF16), wo_ref[...], preferred_element_type=F32)
        h_ref[rows, :] = h
        z_ref[rows, :] = _rmsnorm(h, gn_ref[...]).astype(BF16)


def _mix(attn2, proj2, x2, wa, pw, ps, wp, wo, gn, seq):
    t = x2.shape[0]
    halo_blocks = TM_MIX // POOL_HALO
    n_halo = t // POOL_HALO
    p_blk = P_COL // POOL_WIDTH
    const = dict(pipeline_mode=pl.Buffered(1))
    return pl.pallas_call(
        functools.partial(_mix_body, seq),
        grid=(t // TM_MIX,),
        in_specs=[
            pl.BlockSpec((TM_MIX, ATTN_OUT_WIDTH), lambda i: (i, 0)),
            pl.BlockSpec((TM_MIX, POOL_WIDTH), lambda i: (i, p_blk)),
            pl.BlockSpec((POOL_HALO, POOL_WIDTH),
                         lambda i: (jnp.maximum(i * halo_blocks - 1, 0), p_blk)),
            pl.BlockSpec((POOL_HALO, POOL_WIDTH),
                         lambda i: (jnp.minimum((i + 1) * halo_blocks, n_halo - 1), p_blk)),
            pl.BlockSpec((TM_MIX, D_MODEL), lambda i: (i, 0)),
            pl.BlockSpec((TM_MIX, D_MODEL), lambda i: (i, 1)),
            pl.BlockSpec((TM_MIX, D_MODEL), lambda i: (i, 0)),
            pl.BlockSpec((ATTN_OUT_WIDTH, D_MODEL), lambda i: (0, 0), **const),
            pl.BlockSpec((len(POOL_WINDOWS), POOL_GROUP_WIDTH, POOL_GROUP_WIDTH),
                         lambda i: (0, 0, 0), **const),
            pl.BlockSpec((1, POOL_WIDTH), lambda i: (0, 0)),
            pl.BlockSpec((POOL_WIDTH, D_MODEL), lambda i: (0, 0), **const),
            pl.BlockSpec((D_MODEL, D_MODEL), lambda i: (0, 0), **const),
            pl.BlockSpec((1, D_MODEL), lambda i: (0, 0)),
        ],
        out_specs=[
            pl.BlockSpec((TM_MIX, D_MODEL), lambda i: (i, 0)),
            pl.BlockSpec((TM_MIX, D_MODEL), lambda i: (i, 0)),
        ],
        out_shape=[
            jax.ShapeDtypeStruct((t, D_MODEL), F32),
            jax.ShapeDtypeStruct((t, D_MODEL), BF16),
        ],
        scratch_shapes=[
            pltpu.VMEM((TM_MIX + 2 * POOL_HALO, POOL_WIDTH), F32),
            pltpu.VMEM((TM_MIX, POOL_WIDTH), BF16),
        ],
        compiler_params=pltpu.CompilerParams(
            dimension_semantics=("parallel",), vmem_limit_bytes=VMEM_LIMIT_BYTES),
        name="mix",
    )(attn2, proj2, proj2, proj2, proj2, proj2, x2, wa, pw, ps, wp, wo, gn)


def _ffn_body(seq, z_ref, zprev_ref, znext_ref, h_ref, wa_ref, wb_ref, cw_ref, cb_ref, wd_ref,
              gn_ref, o_ref, zext, aext):
    i = pl.program_id(0)
    f = pl.program_id(1)
    tiles_per_seq = seq // TM_FFN
    ti = i % tiles_per_seq

    @pl.when(f == 0)
    def _():
        zero_rows = jnp.zeros((CONV_HALO, D_MODEL), BF16)
        zext[0:CONV_HALO, :] = jnp.where(ti == 0, zero_rows, zprev_ref[...])
        zext[CONV_HALO: CONV_HALO + TM_FFN, :] = z_ref[...]
        zext[CONV_HALO + TM_FFN:, :] = jnp.where(ti == tiles_per_seq - 1, zero_rows, znext_ref[...])
        o_ref[...] = h_ref[...]

    aext[...] = jnp.dot(zext[...], wa_ref[...], preferred_element_type=F32)
    b = jnp.dot(zext[CONV_HALO: CONV_HALO + TM_FFN, :], wb_ref[...], preferred_element_type=F32)
    cw = cw_ref[...]
    a = (aext[CONV_HALO - 1: CONV_HALO - 1 + TM_FFN, :] * cw[0:1, :]
         + aext[CONV_HALO: CONV_HALO + TM_FFN, :] * cw[1:2, :]
         + aext[CONV_HALO + 1: CONV_HALO + 1 + TM_FFN, :] * cw[2:3, :]
         + cb_ref[...])
    gelu = 0.5 * a * (1.0 + lax.erf(a * np.float32(np.sqrt(0.5))))
    o_ref[...] += jnp.dot((gelu * b).astype(BF16), wd_ref[...], preferred_element_type=F32)

    @pl.when(f == pl.num_programs(1) - 1)
    def _():
        o_ref[...] = _rmsnorm(o_ref[...], gn_ref[...])


def _ffn(z2, h2, w_up, cw, cb, w_down, gn, seq):
    t = z2.shape[0]
    halo_blocks = TM_FFN // CONV_HALO
    n_halo = t // CONV_HALO
    nf = D_FF // TF_FFN
    return pl.pallas_call(
        functools.partial(_ffn_body, seq),
        grid=(t // TM_FFN, nf),
        in_specs=[
            pl.BlockSpec((TM_FFN, D_MODEL), lambda i, f: (i, 0)),
            pl.BlockSpec((CONV_HALO, D_MODEL), lambda i, f: (jnp.maximum(i * halo_blocks - 1, 0), 0)),
            pl.BlockSpec((CONV_HALO, D_MODEL),
                         lambda i, f: (jnp.minimum((i + 1) * halo_blocks, n_halo - 1), 0)),
            pl.BlockSpec((TM_FFN, D_MODEL), lambda i, f: (i, 0)),
            pl.BlockSpec((D_MODEL, TF_FFN), lambda i, f: (0, f)),
            pl.BlockSpec((D_MODEL, TF_FFN), lambda i, f: (0, nf + f)),
            pl.BlockSpec((3, TF_FFN), lambda i, f: (0, f)),
            pl.BlockSpec((1, TF_FFN), lambda i, f: (0, f)),
            pl.BlockSpec((TF_FFN, D_MODEL), lambda i, f: (f, 0)),
            pl.BlockSpec((1, D_MODEL), lambda i, f: (0, 0)),
        ],
        out_specs=pl.BlockSpec((TM_FFN, D_MODEL), lambda i, f: (i, 0)),
        out_shape=jax.ShapeDtypeStruct((t, D_MODEL), F32),
        scratch_shapes=[
            pltpu.VMEM((TM_FFN + 2 * CONV_HALO, D_MODEL), BF16),
            pltpu.VMEM((TM_FFN + 2 * CONV_HALO, TF_FFN), F32),
        ],
        compiler_params=pltpu.CompilerParams(
            dimension_semantics=("parallel", "arbitrary"), vmem_limit_bytes=VMEM_LIMIT_BYTES),
        name="ffn",
    )(z2, z2, z2, h2, w_up, w_up, cw, cb, w_down, gn)


def _rotary_tables(seq):
    inv_freq = ROPE_THETA ** (-np.arange(0, ROT_DIM, 2, dtype=np.float64) / ROT_DIM)
    ang = np.arange(seq, dtype=np.float64)[:, None] * inv_freq[None, :]
    cos, sin = np.cos(ang), np.sin(ang)
    ones = np.ones((seq, HEAD_DIM - ROT_DIM))
    zeros_h = np.zeros((seq, ROT_HALF))
    zeros_r = np.zeros((seq, HEAD_DIM - ROT_DIM))
    cos_t = np.concatenate([cos, cos, ones], axis=-1)
    sina_t = np.concatenate([-sin, zeros_h, zeros_r], axis=-1)
    sinb_t = np.concatenate([zeros_h, sin, zeros_r], axis=-1)
    return tuple(jnp.asarray(t, dtype=F32) for t in (cos_t, sina_t, sinb_t))


def kernel(x, norm_mix_g, w_in, w_attn_out, pool_w, pool_scale, w_pool_out, w_gate, w_out,
           norm_ffn_g, w_up, conv_w, conv_b, w_down, norm_final_g):
    b, seq, d = x.shape
    assert d == D_MODEL and seq % TM_IN == 0 and seq % TM_FFN == 0 and seq % TM_MIX == 0
    assert all(seq % (dil * ATTN_QB) == 0 for _, dil in DILATION_PAIRS)
    t = b * seq
    x2 = x.reshape(t, d)

    cos_t, sina_t, sinb_t = _rotary_tables(seq)
    proj = _inproj(x2, norm_mix_g.reshape(1, d), w_gate.astype(BF16), w_in.astype(BF16),
                   cos_t, sina_t, sinb_t, seq)
    attn, w_up_b, w_down_b, w_attn_out_b, w_pool_out_b, w_out_b = _attention(
        proj.reshape(b, seq, PROJ_COLS), w_up, w_down, (w_attn_out, w_pool_out, w_out))
    h, z = _mix(attn.reshape(t, ATTN_OUT_WIDTH), proj, x2,
                w_attn_out_b, pool_w.astype(BF16), pool_scale.reshape(1, POOL_WIDTH),
                w_pool_out_b, w_out_b, norm_ffn_g.reshape(1, d), seq)
    out = _ffn(z, h, w_up_b, conv_w, conv_b.reshape(1, D_FF), w_down_b,
               norm_final_g.reshape(1, d), seq)
    return out.reshape(b, seq, d)
```

```python
import functools

import numpy as np
import jax
import jax.numpy as jnp
from jax import lax
from jax.experimental import pallas as pl
from jax.experimental.pallas import tpu as pltpu

F32 = jnp.float32
BF16 = jnp.bfloat16

D_MODEL = 2048
HEAD_DIM = 128
DILATION_PAIRS = ((128, 1), (512, 4), (2048, 16))
N_GROUPS = len(DILATION_PAIRS)
HEADS_PER_GROUP = 4
N_ATTN_HEADS = N_GROUPS * HEADS_PER_GROUP
ATTN_WIDTH = N_ATTN_HEADS * HEAD_DIM
ATTN_OUT_WIDTH = HEADS_PER_GROUP * HEAD_DIM
ROT_DIM = HEAD_DIM // 4
ROT_HALF = ROT_DIM // 2
ROPE_THETA = 500000.0
POOL_WINDOWS = (2, 4, 8, 16)
POOL_WIDTH = 512
POOL_GROUP_WIDTH = 128
D_FF = 3 * D_MODEL
NORM_EPS = 1e-6
NEG_INF = -1e30
RADIUS = 64
assert all(w // 2 // d == RADIUS for w, d in DILATION_PAIRS)

LANES = 128
BF16_SUBLANES = 16
MXU_COLS = 256
VMEM_LIMIT_BYTES = 56 * 1024 * 1024

GATE_COLS = 2 * D_MODEL
Q_COL = GATE_COLS
K_COL = Q_COL + ATTN_WIDTH
V_COL = K_COL + ATTN_WIDTH
P_COL = V_COL + ATTN_WIDTH
PROJ_COLS = P_COL + POOL_WIDTH

TM_IN, TN_IN = 1024, 1024
IN_ROW_CHUNKS = 4
ROT_TILE_LO = GATE_COLS // TN_IN
ROT_TILE_HI = V_COL // TN_IN
assert GATE_COLS % TN_IN == 0 and V_COL % TN_IN == 0 and PROJ_COLS % TN_IN == 0
ATTN_QB = 128
ATTN_KW = ATTN_QB + 2 * RADIUS
ATTN_UNROLL = 32
ATTN_PERM_ROWS = 256
ATTN_MERGE_ROWS = 256
CAST_ROWS = 128
TM_MIX = 512
MIX_ROW_CHUNKS = 2
POOL_HALO = BF16_SUBLANES
TM_FFN, TF_FFN = 512, 1024
CONV_HALO = BF16_SUBLANES


def _rmsnorm(x, g):
    return x * lax.rsqrt(jnp.mean(x * x, axis=-1, keepdims=True) + NORM_EPS) * g


def _inproj_body(x_ref, g_ref, wg_ref, wi_ref, cos_ref, sina_ref, sinb_ref, o_ref, u_ref):
    j = pl.program_id(1)

    def chunked(w_ref, epilogue):
        for c in range(IN_ROW_CHUNKS):
            rows = slice(c * (TM_IN // IN_ROW_CHUNKS), (c + 1) * (TM_IN // IN_ROW_CHUNKS))
            acc = jnp.dot(u_ref[rows, :], w_ref[...], preferred_element_type=F32)
            epilogue(acc, rows)

    def sigmoid(acc, rows):
        o_ref[rows, :] = jax.nn.sigmoid(acc).astype(BF16)

    def rotary(acc, rows):
        c = cos_ref[rows, :]
        sa = sina_ref[rows, :]
        sb = sinb_ref[rows, :]
        for hd in range(TN_IN // HEAD_DIM):
            sl = slice(hd * HEAD_DIM, (hd + 1) * HEAD_DIM)
            blk = acc[:, sl]
            up = pltpu.roll(blk, HEAD_DIM - ROT_HALF, axis=1)
            dn = pltpu.roll(blk, ROT_HALF, axis=1)
            o_ref[rows, sl] = (blk * c + up * sa + dn * sb).astype(BF16)

    def plain(acc, rows):
        o_ref[rows, :] = acc.astype(BF16)

    @pl.when(j == 0)
    def _():
        for c in range(IN_ROW_CHUNKS):
            rows = slice(c * (TM_IN // IN_ROW_CHUNKS), (c + 1) * (TM_IN // IN_ROW_CHUNKS))
            u = _rmsnorm(x_ref[rows, :], g_ref[...]).astype(BF16)
            u_ref[rows, :] = u
            sigmoid(jnp.dot(u, wg_ref[...], preferred_element_type=F32), rows)

    @pl.when((j > 0) & (j < ROT_TILE_LO))
    def _():
        chunked(wg_ref, sigmoid)

    @pl.when((j >= ROT_TILE_LO) & (j < ROT_TILE_HI))
    def _():
        chunked(wi_ref, rotary)

    @pl.when(j >= ROT_TILE_HI)
    def _():
        chunked(wi_ref, plain)


def _inproj(x2, g, w_gate, w_in, cos_t, sina_t, sinb_t, seq):
    t = x2.shape[0]
    pos_tiles = seq // TM_IN
    gate_tiles = GATE_COLS // TN_IN
    table_spec = pl.BlockSpec((TM_IN, HEAD_DIM), lambda i, j: (i % pos_tiles, 0))
    return pl.pallas_call(
        _inproj_body,
        grid=(t // TM_IN, PROJ_COLS // TN_IN),
        in_specs=[
            pl.BlockSpec((TM_IN, D_MODEL), lambda i, j: (i, 0)),
            pl.BlockSpec((1, D_MODEL), lambda i, j: (0, 0)),
            pl.BlockSpec((D_MODEL, TN_IN), lambda i, j: (0, jnp.minimum(j, gate_tiles - 1))),
            pl.BlockSpec((D_MODEL, TN_IN), lambda i, j: (0, jnp.maximum(j - gate_tiles, 0))),
            table_spec, table_spec, table_spec,
        ],
        out_specs=pl.BlockSpec((TM_IN, TN_IN), lambda i, j: (i, j)),
        out_shape=jax.ShapeDtypeStruct((t, PROJ_COLS), BF16),
        scratch_shapes=[pltpu.VMEM((TM_IN, D_MODEL), BF16)],
        compiler_params=pltpu.CompilerParams(
            dimension_semantics=("parallel", "arbitrary"), vmem_limit_bytes=VMEM_LIMIT_BYTES),
        name="inproj",
    )(x2, g, w_gate, w_in, cos_t, sina_t, sinb_t)


def _perm_matrices():
    mats = []
    for _, d in DILATION_PAIRS:
        seg = ATTN_PERM_ROWS // d
        p = np.zeros((ATTN_PERM_ROWS, ATTN_PERM_ROWS), np.float32)
        for r in range(d):
            for l in range(seg):
                p[r * seg + l, l * d + r] = 1.0
        mats.append(p)
    return jnp.asarray(np.stack(mats), dtype=BF16)


def _attn_group(gi, d, seq, q_ref, k_ref, v_ref, perm_ref, qd, kd, vd, mask, bias, og, lg):
    L = seq // d
    LP = L + 2 * RADIUS
    nqb = L // ATTN_QB
    seg = ATTN_PERM_ROWS // d

    def deinterleave(src_ref, dst_ref, padded):
        for c in range(seq // ATTN_PERM_ROWS):
            x = src_ref[c * ATTN_PERM_ROWS: (c + 1) * ATTN_PERM_ROWS, :]
            y = jnp.dot(perm_ref[gi], x, preferred_element_type=F32).astype(BF16)
            for r in range(d):
                base = (r * LP + RADIUS if padded else r * L) + c * seg
                dst_ref[base: base + seg, 0:HEAD_DIM] = y[r * seg: (r + 1) * seg, :]

    zero_pad = jnp.zeros((RADIUS, HEAD_DIM), BF16)
    for r in range(d):
        for dst in (kd, vd):
            dst[r * LP: r * LP + RADIUS, 0:HEAD_DIM] = zero_pad
            dst[r * LP + RADIUS + L: (r + 1) * LP, 0:HEAD_DIM] = zero_pad
    vd[0: d * LP, HEAD_DIM:] = jnp.ones((d * LP, HEAD_DIM), BF16)
    if d == 1:
        q_src = q_ref
        kd[RADIUS: RADIUS + seq, :] = k_ref[...]
        vd[RADIUS: RADIUS + seq, 0:HEAD_DIM] = v_ref[...]
    else:
        q_src = qd
        deinterleave(q_ref, qd, False)
        deinterleave(k_ref, kd, True)
        deinterleave(v_ref, vd, True)

    scale = HEAD_DIM ** -0.5

    def body(blk, carry):
        r = blk // nqb
        qb = blk % nqb
        q0 = pl.multiple_of(r * L + qb * ATTN_QB, ATTN_QB)
        k0 = pl.multiple_of(r * LP + qb * ATTN_QB, ATTN_QB)
        q = q_src[pl.ds(q0, ATTN_QB), :]
        kw = kd[pl.ds(k0, ATTN_KW), :]
        vw = vd[pl.ds(k0, ATTN_KW), :]
        kind = jnp.where(qb == 0, 1, jnp.where(qb == nqb - 1, 2, 0))
        s = lax.dot_general(q, kw, (((1,), (1,)), ((), ())), preferred_element_type=F32)
        s = s * scale + bias[kind]
        m = jnp.max(s, axis=-1, keepdims=True)
        p = jnp.exp(s - m) * mask[kind]
        pv = jnp.dot(p.astype(BF16), vw, preferred_element_type=F32)
        den = pv[:, HEAD_DIM:]
        o = pv[:, :HEAD_DIM] / den
        lse = m + jnp.log(den)
        if d == 1:
            t0 = pl.multiple_of(qb * ATTN_QB, ATTN_QB)
            og[gi][pl.ds(t0, ATTN_QB), :] = o
            lg[gi][pl.ds(t0, ATTN_QB), :] = lse
        else:
            t0 = qb * (ATTN_QB * d) + r
            og[gi][pl.ds(t0, ATTN_QB, stride=d), :] = o
            lg[gi][pl.ds(t0, ATTN_QB, stride=d), :] = lse
        return carry

    lax.fori_loop(0, seq // ATTN_QB, body, 0, unroll=ATTN_UNROLL)


def _attn_body(seq, n_cast, q_ref, k_ref, v_ref, perm_ref, *refs):
    cast_in, o_ref, cast_out = refs[:n_cast], refs[n_cast], refs[n_cast + 1: 2 * n_cast + 1]
    qd, kd, vd, mask, bias, og0, og1, og2, lg0, lg1, lg2 = refs[2 * n_cast + 1:]
    g = pl.program_id(2)
    og = (og0, og1, og2)
    lg = (lg0, lg1, lg2)

    @pl.when(g > 0)
    def _():
        for src, dst in zip(cast_in, cast_out):
            dst[...] = src[...].astype(BF16)

    row = lax.broadcasted_iota(jnp.int32, (ATTN_QB, ATTN_KW), 0)
    col = lax.broadcasted_iota(jnp.int32, (ATTN_QB, ATTN_KW), 1)
    band = (col >= row) & (col <= row + 2 * RADIUS)
    for kind, valid in enumerate((band, band & (col >= RADIUS), band & (col < ATTN_QB + RADIUS))):
        mask[kind] = jnp.where(valid, 1.0, 0.0)
        bias[kind] = jnp.where(valid, 0.0, NEG_INF)

    for gi, (_, d) in enumerate(DILATION_PAIRS):
        @pl.when(g == gi)
        def _(gi=gi, d=d):
            _attn_group(gi, d, seq, q_ref, k_ref, v_ref, perm_ref, qd, kd, vd, mask, bias, og, lg)

    @pl.when(g == N_GROUPS - 1)
    def _():
        for c in range(seq // ATTN_MERGE_ROWS):
            sl = slice(c * ATTN_MERGE_ROWS, (c + 1) * ATTN_MERGE_ROWS)
            l0, l1, l2 = lg0[sl, :], lg1[sl, :], lg2[sl, :]
            mx = jnp.maximum(jnp.maximum(l0, l1), l2)
            e0, e1, e2 = jnp.exp(l0 - mx), jnp.exp(l1 - mx), jnp.exp(l2 - mx)
            num = e0 * og0[sl, :] + e1 * og1[sl, :] + e2 * og2[sl, :]
            o_ref[sl, :] = (num / (e0 + e1 + e2)).astype(BF16)


def _attention(proj3, w_up, w_down, row_sliced):
    b, seq, _ = proj3.shape
    max_d = max(d for _, d in DILATION_PAIRS)
    steps = b * HEADS_PER_GROUP * (N_GROUPS - 1)
    up_cols = w_up.shape[1] // steps
    down_rows = w_down.shape[0] // steps
    assert w_up.shape[1] % (steps * LANES) == 0 and w_down.shape[0] % (steps * BF16_SUBLANES) == 0

    def head_spec(col0):
        blk0 = col0 // HEAD_DIM
        return pl.BlockSpec((None, seq, HEAD_DIM),
                            lambda bi, hs, g: (bi, 0, blk0 + g * HEADS_PER_GROUP + hs))

    def step(bi, hs, g):
        return (bi * HEADS_PER_GROUP + hs) * (N_GROUPS - 1) + jnp.maximum(g - 1, 0)

    def rows_spec(w):
        n = w.shape[0] // CAST_ROWS
        assert w.shape[0] % CAST_ROWS == 0 and n <= steps
        return pl.BlockSpec((CAST_ROWS, w.shape[1]),
                            lambda bi, hs, g: (jnp.minimum(step(bi, hs, g), n - 1), 0))

    cast_specs = [
        pl.BlockSpec((w_up.shape[0], up_cols), lambda bi, hs, g: (0, step(bi, hs, g))),
        pl.BlockSpec((down_rows, w_down.shape[1]), lambda bi, hs, g: (step(bi, hs, g), 0)),
    ] + [rows_spec(w) for w in row_sliced]
    cast_args = [w_up, w_down] + list(row_sliced)
    tok = pltpu.VMEM((seq, HEAD_DIM), F32)
    win = pltpu.VMEM((3, ATTN_QB, ATTN_KW), F32)
    return pl.pallas_call(
        functools.partial(_attn_body, seq, len(cast_args)),
        grid=(b, HEADS_PER_GROUP, N_GROUPS),
        in_specs=[head_spec(Q_COL), head_spec(K_COL), head_spec(V_COL),
                  pl.BlockSpec((N_GROUPS, ATTN_PERM_ROWS, ATTN_PERM_ROWS),
                               lambda bi, hs, g: (0, 0, 0))] + cast_specs,
        out_specs=[pl.BlockSpec((None, seq, HEAD_DIM), lambda bi, hs, g: (bi, 0, hs))] + cast_specs,
        out_shape=[jax.ShapeDtypeStruct((b, seq, ATTN_OUT_WIDTH), BF16)]
        + [jax.ShapeDtypeStruct(w.shape, BF16) for w in cast_args],
        scratch_shapes=[
            pltpu.VMEM((seq, HEAD_DIM), BF16),
            pltpu.VMEM((seq + 2 * RADIUS * max_d, HEAD_DIM), BF16),
            pltpu.VMEM((seq + 2 * RADIUS * max_d, 2 * HEAD_DIM), BF16),
            win, win,
            tok, tok, tok, tok, tok, tok,
        ],
        compiler_params=pltpu.CompilerParams(
            dimension_semantics=("arbitrary", "arbitrary", "arbitrary"),
            vmem_limit_bytes=VMEM_LIMIT_BYTES),
        name="attn",
    )(proj3, proj3, proj3, _perm_matrices(), *cast_args)


def _mix_body(seq, attn_ref, p_ref, pprev_ref, pnext_ref, ga_ref, gb_ref, x_ref,
              wa_ref, pw_ref, ps_ref, wp_ref, wo_ref, gn_ref, h_ref, z_ref, pad, pm):
    i = pl.program_id(0)
    tiles_per_seq = seq // TM_MIX
    ti = i % tiles_per_seq
    pad[0:POOL_HALO, :] = jnp.where(ti == 0, 0.0, pprev_ref[...].astype(F32))
    pad[POOL_HALO: POOL_HALO + TM_MIX, :] = p_ref[...].astype(F32)
    pad[POOL_HALO + TM_MIX:, :] = jnp.where(ti == tiles_per_seq - 1, 0.0, pnext_ref[...].astype(F32))

    ya_cols = D_MODEL // len(POOL_WINDOWS)
    ya_parts = []
    t = ti * TM_MIX + lax.broadcasted_iota(jnp.int32, (TM_MIX, POOL_GROUP_WIDTH), 0)
    for g, w in enumerate(POOL_WINDOWS):
        ya_parts.append(jnp.dot(attn_ref[...], wa_ref[:, g * ya_cols: (g + 1) * ya_cols],
                                preferred_element_type=F32))
        cs = slice(g * POOL_GROUP_WIDTH, (g + 1) * POOL_GROUP_WIDTH)
        tot = pad[POOL_HALO - w // 2: POOL_HALO - w // 2 + TM_MIX, cs]
        for j in range(-w // 2 + 1, w // 2):
            tot = tot + pad[POOL_HALO + j: POOL_HALO + j + TM_MIX, cs]
        cnt = (jnp.minimum(t + w // 2, seq) - jnp.maximum(t - w // 2, 0)).astype(F32)
        pooled = tot / cnt - pad[POOL_HALO: POOL_HALO + TM_MIX, cs]
        pmg = jnp.dot(pooled.astype(BF16), pw_ref[g], preferred_element_type=F32) * ps_ref[:, cs]
        pm[:, cs] = pmg.astype(BF16)

    y_a = jnp.concatenate(ya_parts, axis=1)
    y_b = jnp.dot(pm[...], wp_ref[...], preferred_element_type=F32)
    rc = TM_MIX // MIX_ROW_CHUNKS
    for c in range(MIX_ROW_CHUNKS):
        rows = slice(c * rc, (c + 1) * rc)
        mixed = ga_ref[rows, :].astype(F32) * y_a[rows] + gb_ref[rows, :].astype(F32) * y_b[rows]
        h = x_ref[rows, :] + jnp.dot(mixed.astype(BF16), wo_ref[...], preferred_element_type=F32)
        h_ref[rows, :] = h
        z_ref[rows, :] = _rmsnorm(h, gn_ref[...]).astype(BF16)


def _mix(attn2, proj2, x2, wa, pw, ps, wp, wo, gn, seq):
    t = x2.shape[0]
    halo_blocks = TM_MIX // POOL_HALO
    n_halo = t // POOL_HALO
    p_blk = P_COL // POOL_WIDTH
    const = dict(pipeline_mode=pl.Buffered(1))
    return pl.pallas_call(
        functools.partial(_mix_body, seq),
        grid=(t // TM_MIX,),
        in_specs=[
            pl.BlockSpec((TM_MIX, ATTN_OUT_WIDTH), lambda i: (i, 0)),
            pl.BlockSpec((TM_MIX, POOL_WIDTH), lambda i: (i, p_blk)),
            pl.BlockSpec((POOL_HALO, POOL_WIDTH),
                         lambda i: (jnp.maximum(i * halo_blocks - 1, 0), p_blk)),
            pl.BlockSpec((POOL_HALO, POOL_WIDTH),
                         lambda i: (jnp.minimum((i + 1) * halo_blocks, n_halo - 1), p_blk)),
            pl.BlockSpec((TM_MIX, D_MODEL), lambda i: (i, 0)),
            pl.BlockSpec((TM_MIX, D_MODEL), lambda i: (i, 1)),
            pl.BlockSpec((TM_MIX, D_MODEL), lambda i: (i, 0)),
            pl.BlockSpec((ATTN_OUT_WIDTH, D_MODEL), lambda i: (0, 0), **const),
            pl.BlockSpec((len(POOL_WINDOWS), POOL_GROUP_WIDTH, POOL_GROUP_WIDTH),
                         lambda i: (0, 0, 0), **const),
            pl.BlockSpec((1, POOL_WIDTH), lambda i: (0, 0)),
            pl.BlockSpec((POOL_WIDTH, D_MODEL), lambda i: (0, 0), **const),
            pl.BlockSpec((D_MODEL, D_MODEL), lambda i: (0, 0), **const),
            pl.BlockSpec((1, D_MODEL), lambda i: (0, 0)),
        ],
        out_specs=[
            pl.BlockSpec((TM_MIX, D_MODEL), lambda i: (i, 0)),
            pl.BlockSpec((TM_MIX, D_MODEL), lambda i: (i, 0)),
        ],
        out_shape=[
            jax.ShapeDtypeStruct((t, D_MODEL), F32),
            jax.ShapeDtypeStruct((t, D_MODEL), BF16),
        ],
        scratch_shapes=[
            pltpu.VMEM((TM_MIX + 2 * POOL_HALO, POOL_WIDTH), F32),
            pltpu.VMEM((TM_MIX, POOL_WIDTH), BF16),
        ],
        compiler_params=pltpu.CompilerParams(
            dimension_semantics=("parallel",), vmem_limit_bytes=VMEM_LIMIT_BYTES),
        name="mix",
    )(attn2, proj2, proj2, proj2, proj2, proj2, x2, wa, pw, ps, wp, wo, gn)


def _ffn_body(seq, z_ref, zprev_ref, znext_ref, h_ref, wa_ref, wb_ref, cw_ref, cb_ref, wd_ref,
              gn_ref, o_ref, zext, aext):
    i = pl.program_id(0)
    f = pl.program_id(1)
    tiles_per_seq = seq // TM_FFN
    ti = i % tiles_per_seq

    @pl.when(f == 0)
    def _():
        zero_rows = jnp.zeros((CONV_HALO, D_MODEL), BF16)
        zext[0:CONV_HALO, :] = jnp.where(ti == 0, zero_rows, zprev_ref[...])
        zext[CONV_HALO: CONV_HALO + TM_FFN, :] = z_ref[...]
        zext[CONV_HALO + TM_FFN:, :] = jnp.where(ti == tiles_per_seq - 1, zero_rows, znext_ref[...])
        o_ref[...] = h_ref[...]

    aext[...] = jnp.dot(zext[...], wa_ref[...], preferred_element_type=F32)
    b = jnp.dot(zext[CONV_HALO: CONV_HALO + TM_FFN, :], wb_ref[...], preferred_element_type=F32)
    cw = cw_ref[...]
    a = (aext[CONV_HALO - 1: CONV_HALO - 1 + TM_FFN, :] * cw[0:1, :]
         + aext[CONV_HALO: CONV_HALO + TM_FFN, :] * cw[1:2, :]
         + aext[CONV_HALO + 1: CONV_HALO + 1 + TM_FFN, :] * cw[2:3, :]
         + cb_ref[...])
    gelu = 0.5 * a * (1.0 + lax.erf(a * np.float32(np.sqrt(0.5))))
    o_ref[...] += jnp.dot((gelu * b).astype(BF16), wd_ref[...], preferred_element_type=F32)

    @pl.when(f == pl.num_programs(1) - 1)
    def _():
        o_ref[...] = _rmsnorm(o_ref[...], gn_ref[...])


def _ffn(z2, h2, w_up, cw, cb, w_down, gn, seq):
    t = z2.shape[0]
    halo_blocks = TM_FFN // CONV_HALO
    n_halo = t // CONV_HALO
    nf = D_FF // TF_FFN
    return pl.pallas_call(
        functools.partial(_ffn_body, seq),
        grid=(t // TM_FFN, nf),
        in_specs=[
            pl.BlockSpec((TM_FFN, D_MODEL), lambda i, f: (i, 0)),
            pl.BlockSpec((CONV_HALO, D_MODEL), lambda i, f: (jnp.maximum(i * halo_blocks - 1, 0), 0)),
            pl.BlockSpec((CONV_HALO, D_MODEL),
                         lambda i, f: (jnp.minimum((i + 1) * halo_blocks, n_halo - 1), 0)),
            pl.BlockSpec((TM_FFN, D_MODEL), lambda i, f: (i, 0)),
            pl.BlockSpec((D_MODEL, TF_FFN), lambda i, f: (0, f)),
            pl.BlockSpec((D_MODEL, TF_FFN), lambda i, f: (0, nf + f)),
            pl.BlockSpec((3, TF_FFN), lambda i, f: (0, f)),
            pl.BlockSpec((1, TF_FFN), lambda i, f: (0, f)),
            pl.BlockSpec((TF_FFN, D_MODEL), lambda i, f: (f, 0)),
            pl.BlockSpec((1, D_MODEL), lambda i, f: (0, 0)),
        ],
        out_specs=pl.BlockSpec((TM_FFN, D_MODEL), lambda i, f: (i, 0)),
        out_shape=jax.ShapeDtypeStruct((t, D_MODEL), F32),
        scratch_shapes=[
            pltpu.VMEM((TM_FFN + 2 * CONV_HALO, D_MODEL), BF16),
            pltpu.VMEM((TM_FFN + 2 * CONV_HALO, TF_FFN), F32),
        ],
        compiler_params=pltpu.CompilerParams(
            dimension_semantics=("parallel", "arbitrary"), vmem_limit_bytes=VMEM_LIMIT_BYTES),
        name="ffn",
    )(z2, z2, z2, h2, w_up, w_up, cw, cb, w_down, gn)


def _rotary_tables(seq):
    inv_freq = ROPE_THETA ** (-np.arange(0, ROT_DIM, 2, dtype=np.float64) / ROT_DIM)
    ang = np.arange(seq, dtype=np.float64)[:, None] * inv_freq[None, :]
    cos, sin = np.cos(ang), np.sin(ang)
    ones = np.ones((seq, HEAD_DIM - ROT_DIM))
    zeros_h = np.zeros((seq, ROT_HALF))
    zeros_r = np.zeros((seq, HEAD_DIM - ROT_DIM))
    cos_t = np.concatenate([cos, cos, ones], axis=-1)
    sina_t = np.concatenate([-sin, zeros_h, zeros_r], axis=-1)
    sinb_t = np.concatenate([zeros_h, sin, zeros_r], axis=-1)
    return tuple(jnp.asarray(t, dtype=F32) for t in (cos_t, sina_t, sinb_t))


def kernel(x, norm_mix_g, w_in, w_attn_out, pool_w, pool_scale, w_pool_out, w_gate, w_out,
           norm_ffn_g, w_up, conv_w, conv_b, w_down, norm_final_g):
    b, seq, d = x.shape
    assert d == D_MODEL and seq % TM_IN == 0 and seq % TM_FFN == 0 and seq % TM_MIX == 0
    assert all(seq % (dil * ATTN_QB) == 0 for _, dil in DILATION_PAIRS)
    t = b * seq
    x2 = x.reshape(t, d)

    cos_t, sina_t, sinb_t = _rotary_tables(seq)
    proj = _inproj(x2, norm_mix_g.reshape(1, d), w_gate.astype(BF16), w_in.astype(BF16),
                   cos_t, sina_t, sinb_t, seq)
    attn, w_up_b, w_down_b, w_attn_out_b, w_pool_out_b, w_out_b = _attention(
        proj.reshape(b, seq, PROJ_COLS), w_up, w_down, (w_attn_out, w_pool_out, w_out))
    h, z = _mix(attn.reshape(t, ATTN_OUT_WIDTH), proj, x2,
                w_attn_out_b, pool_w.astype(BF16), pool_scale.reshape(1, POOL_WIDTH),
                w_pool_out_b, w_out_b, norm_ffn_g.reshape(1, d), seq)
    out = _ffn(z, h, w_up_b, conv_w, conv_b.reshape(1, D_FF), w_down_b,
               norm_final_g.reshape(1, d), seq)
    return out.reshape(b, seq, d)
```

```python
import functools

import numpy as np
import jax
import jax.numpy as jnp
from jax import lax
from jax.experimental import pallas as pl
from jax.experimental.pallas import tpu as pltpu

F32 = jnp.float32
BF16 = jnp.bfloat16

D_MODEL = 2048
HEAD_DIM = 128
DILATION_PAIRS = ((128, 1), (512, 4), (2048, 16))
N_GROUPS = len(DILATION_PAIRS)
HEADS_PER_GROUP = 4
N_ATTN_HEADS = N_GROUPS * HEADS_PER_GROUP
ATTN_WIDTH = N_ATTN_HEADS * HEAD_DIM
ATTN_OUT_WIDTH = HEADS_PER_GROUP * HEAD_DIM
ROT_DIM = HEAD_DIM // 4
ROT_HALF = ROT_DIM // 2
ROPE_THETA = 500000.0
POOL_WINDOWS = (2, 4, 8, 16)
POOL_WIDTH = 512
POOL_GROUP_WIDTH = 128
D_FF = 3 * D_MODEL
NORM_EPS = 1e-6
NEG_INF = -1e30
RADIUS = 64
assert all(w // 2 // d == RADIUS for w, d in DILATION_PAIRS)

LANES = 128
BF16_SUBLANES = 16
MXU_COLS = 256
VMEM_LIMIT_BYTES = 56 * 1024 * 1024

GATE_COLS = 2 * D_MODEL
Q_COL = GATE_COLS
K_COL = Q_COL + ATTN_WIDTH
V_COL = K_COL + ATTN_WIDTH
P_COL = V_COL + ATTN_WIDTH
PROJ_COLS = P_COL + POOL_WIDTH

TM_IN, TN_IN = 1024, 1024
IN_ROW_CHUNKS = 4
ROT_TILE_LO = GATE_COLS // TN_IN
ROT_TILE_HI = V_COL // TN_IN
assert GATE_COLS % TN_IN == 0 and V_COL % TN_IN == 0 and PROJ_COLS % TN_IN == 0
ATTN_QB = 128
ATTN_KW = ATTN_QB + 2 * RADIUS
ATTN_UNROLL = 32
ATTN_PERM_ROWS = 256
ATTN_MERGE_ROWS = 256
CAST_ROWS = 128
TM_MIX = 512
MIX_ROW_CHUNKS = 2
POOL_HALO = BF16_SUBLANES
TM_FFN, TF_FFN = 512, 1024
CONV_HALO = BF16_SUBLANES


def _rmsnorm(x, g):
    return x * lax.rsqrt(jnp.mean(x * x, axis=-1, keepdims=True) + NORM_EPS) * g


def _inproj_body(x_ref, g_ref, wg_ref, wi_ref, cos_ref, sina_ref, sinb_ref, o_ref, u_ref):
    j = pl.program_id(1)

    def chunked(w_ref, epilogue):
        for c in range(IN_ROW_CHUNKS):
            rows = slice(c * (TM_IN // IN_ROW_CHUNKS), (c + 1) * (TM_IN // IN_ROW_CHUNKS))
            acc = jnp.dot(u_ref[rows, :], w_ref[...], preferred_element_type=F32)
            epilogue(acc, rows)

    def store_heads(val, rows):
        for hd in range(TN_IN // HEAD_DIM):
            o_ref[hd, rows, :] = val[:, hd * HEAD_DIM: (hd + 1) * HEAD_DIM].astype(BF16)

    def sigmoid(acc, rows):
        store_heads(jax.nn.sigmoid(acc), rows)

    def rotary(acc, rows):
        c = cos_ref[rows, :]
        sa = sina_ref[rows, :]
        sb = sinb_ref[rows, :]
        for hd in range(TN_IN // HEAD_DIM):
            sl = slice(hd * HEAD_DIM, (hd + 1) * HEAD_DIM)
            blk = acc[:, sl]
            up = pltpu.roll(blk, HEAD_DIM - ROT_HALF, axis=1)
            dn = pltpu.roll(blk, ROT_HALF, axis=1)
            o_ref[hd, rows, :] = (blk * c + up * sa + dn * sb).astype(BF16)

    @pl.when(j == 0)
    def _():
        for c in range(IN_ROW_CHUNKS):
            rows = slice(c * (TM_IN // IN_ROW_CHUNKS), (c + 1) * (TM_IN // IN_ROW_CHUNKS))
            u = _rmsnorm(x_ref[rows, :], g_ref[...]).astype(BF16)
            u_ref[rows, :] = u
            sigmoid(jnp.dot(u, wg_ref[...], preferred_element_type=F32), rows)

    @pl.when((j > 0) & (j < ROT_TILE_LO))
    def _():
        chunked(wg_ref, sigmoid)

    @pl.when((j >= ROT_TILE_LO) & (j < ROT_TILE_HI))
    def _():
        chunked(wi_ref, rotary)

    @pl.when(j >= ROT_TILE_HI)
    def _():
        chunked(wi_ref, store_heads)


def _inproj(x2, g, w_gate, w_in, cos_t, sina_t, sinb_t, seq):
    t = x2.shape[0]
    pos_tiles = seq // TM_IN
    gate_tiles = GATE_COLS // TN_IN
    table_spec = pl.BlockSpec((TM_IN, HEAD_DIM), lambda i, j: (i % pos_tiles, 0))
    return pl.pallas_call(
        _inproj_body,
        grid=(t // TM_IN, PROJ_COLS // TN_IN),
        in_specs=[
            pl.BlockSpec((TM_IN, D_MODEL), lambda i, j: (i, 0)),
            pl.BlockSpec((1, D_MODEL), lambda i, j: (0, 0)),
            pl.BlockSpec((D_MODEL, TN_IN), lambda i, j: (0, jnp.minimum(j, gate_tiles - 1))),
            pl.BlockSpec((D_MODEL, TN_IN), lambda i, j: (0, jnp.maximum(j - gate_tiles, 0))),
            table_spec, table_spec, table_spec,
        ],
        out_specs=pl.BlockSpec((TN_IN // HEAD_DIM, TM_IN, HEAD_DIM), lambda i, j: (j, i, 0)),
        out_shape=jax.ShapeDtypeStruct((PROJ_COLS // HEAD_DIM, t, HEAD_DIM), BF16),
        scratch_shapes=[pltpu.VMEM((TM_IN, D_MODEL), BF16)],
        compiler_params=pltpu.CompilerParams(
            dimension_semantics=("parallel", "arbitrary"), vmem_limit_bytes=VMEM_LIMIT_BYTES),
        name="inproj",
    )(x2, g, w_gate, w_in, cos_t, sina_t, sinb_t)


def _perm_matrices():
    mats = []
    for _, d in DILATION_PAIRS:
        seg = ATTN_PERM_ROWS // d
        p = np.zeros((ATTN_PERM_ROWS, ATTN_PERM_ROWS), np.float32)
        for r in range(d):
            for l in range(seg):
                p[r * seg + l, l * d + r] = 1.0
        mats.append(p)
    return jnp.asarray(np.stack(mats), dtype=BF16)


def _attn_group(gi, d, seq, q_ref, k_ref, v_ref, perm_ref, qd, kd, vd, mask, bias, og, lg):
    L = seq // d
    LP = L + 2 * RADIUS
    nqb = L // ATTN_QB
    seg = ATTN_PERM_ROWS // d

    def deinterleave(src_ref, dst_ref, padded):
        for c in range(seq // ATTN_PERM_ROWS):
            x = src_ref[c * ATTN_PERM_ROWS: (c + 1) * ATTN_PERM_ROWS, :]
            y = jnp.dot(perm_ref[gi], x, preferred_element_type=F32).astype(BF16)
            for r in range(d):
                base = (r * LP + RADIUS if padded else r * L) + c * seg
                dst_ref[base: base + seg, 0:HEAD_DIM] = y[r * seg: (r + 1) * seg, :]

    zero_pad = jnp.zeros((RADIUS, HEAD_DIM), BF16)
    for r in range(d):
        for dst in (kd, vd):
            dst[r * LP: r * LP + RADIUS, 0:HEAD_DIM] = zero_pad
            dst[r * LP + RADIUS + L: (r + 1) * LP, 0:HEAD_DIM] = zero_pad
    vd[0: d * LP, HEAD_DIM:] = jnp.ones((d * LP, HEAD_DIM), BF16)
    if d == 1:
        q_src = q_ref
        kd[RADIUS: RADIUS + seq, :] = k_ref[...]
        vd[RADIUS: RADIUS + seq, 0:HEAD_DIM] = v_ref[...]
    else:
        q_src = qd
        deinterleave(q_ref, qd, False)
        deinterleave(k_ref, kd, True)
        deinterleave(v_ref, vd, True)

    scale = HEAD_DIM ** -0.5

    def body(blk, carry):
        r = blk // nqb
        qb = blk % nqb
        q0 = pl.multiple_of(r * L + qb * ATTN_QB, ATTN_QB)
        k0 = pl.multiple_of(r * LP + qb * ATTN_QB, ATTN_QB)
        q = q_src[pl.ds(q0, ATTN_QB), :]
        kw = kd[pl.ds(k0, ATTN_KW), :]
        vw = vd[pl.ds(k0, ATTN_KW), :]
        kind = jnp.where(qb == 0, 1, jnp.where(qb == nqb - 1, 2, 0))
        s = lax.dot_general(q, kw, (((1,), (1,)), ((), ())), preferred_element_type=F32)
        s = s * scale + bias[kind]
        m = jnp.max(s, axis=-1, keepdims=True)
        p = jnp.exp(s - m) * mask[kind]
        pv = jnp.dot(p.astype(BF16), vw, preferred_element_type=F32)
        den = pv[:, HEAD_DIM:]
        o = pv[:, :HEAD_DIM] / den
        lse = m + jnp.log(den)
        if d == 1:
            t0 = pl.multiple_of(qb * ATTN_QB, ATTN_QB)
            og[gi][pl.ds(t0, ATTN_QB), :] = o
            lg[gi][pl.ds(t0, ATTN_QB), :] = lse
        else:
            t0 = qb * (ATTN_QB * d) + r
            og[gi][pl.ds(t0, ATTN_QB, stride=d), :] = o
            lg[gi][pl.ds(t0, ATTN_QB, stride=d), :] = lse
        return carry

    lax.fori_loop(0, seq // ATTN_QB, body, 0, unroll=ATTN_UNROLL)


def _attn_body(seq, n_cast, q_ref, k_ref, v_ref, perm_ref, *refs):
    cast_in, o_ref, cast_out = refs[:n_cast], refs[n_cast], refs[n_cast + 1: 2 * n_cast + 1]
    qd, kd, vd, mask, bias, og0, og1, og2, lg0, lg1, lg2 = refs[2 * n_cast + 1:]
    g = pl.program_id(2)
    og = (og0, og1, og2)
    lg = (lg0, lg1, lg2)

    @pl.when(g > 0)
    def _():
        for src, dst in zip(cast_in, cast_out):
            dst[...] = src[...].astype(BF16)

    row = lax.broadcasted_iota(jnp.int32, (ATTN_QB, ATTN_KW), 0)
    col = lax.broadcasted_iota(jnp.int32, (ATTN_QB, ATTN_KW), 1)
    band = (col >= row) & (col <= row + 2 * RADIUS)
    for kind, valid in enumerate((band, band & (col >= RADIUS), band & (col < ATTN_QB + RADIUS))):
        mask[kind] = jnp.where(valid, 1.0, 0.0)
        bias[kind] = jnp.where(valid, 0.0, NEG_INF)

    for gi, (_, d) in enumerate(DILATION_PAIRS):
        @pl.when(g == gi)
        def _(gi=gi, d=d):
            _attn_group(gi, d, seq, q_ref, k_ref, v_ref, perm_ref, qd, kd, vd, mask, bias, og, lg)

    @pl.when(g == N_GROUPS - 1)
    def _():
        for c in range(seq // ATTN_MERGE_ROWS):
            sl = slice(c * ATTN_MERGE_ROWS, (c + 1) * ATTN_MERGE_ROWS)
            l0, l1, l2 = lg0[sl, :], lg1[sl, :], lg2[sl, :]
            mx = jnp.maximum(jnp.maximum(l0, l1), l2)
            e0, e1, e2 = jnp.exp(l0 - mx), jnp.exp(l1 - mx), jnp.exp(l2 - mx)
            num = e0 * og0[sl, :] + e1 * og1[sl, :] + e2 * og2[sl, :]
            o_ref[sl, :] = (num / (e0 + e1 + e2)).astype(BF16)


def _attention(proj_h, b, seq, w_up, w_down, row_sliced):
    max_d = max(d for _, d in DILATION_PAIRS)
    steps = b * HEADS_PER_GROUP * (N_GROUPS - 1)
    up_cols = w_up.shape[1] // steps
    down_rows = w_down.shape[0] // steps
    assert w_up.shape[1] % (steps * LANES) == 0 and w_down.shape[0] % (steps * BF16_SUBLANES) == 0

    def head_spec(col0):
        blk0 = col0 // HEAD_DIM
        return pl.BlockSpec((None, seq, HEAD_DIM),
                            lambda bi, hs, g: (blk0 + g * HEADS_PER_GROUP + hs, bi, 0))

    def step(bi, hs, g):
        return (bi * HEADS_PER_GROUP + hs) * (N_GROUPS - 1) + jnp.maximum(g - 1, 0)

    def rows_spec(w):
        n = w.shape[0] // CAST_ROWS
        assert w.shape[0] % CAST_ROWS == 0 and n <= steps
        return pl.BlockSpec((CAST_ROWS, w.shape[1]),
                            lambda bi, hs, g: (jnp.minimum(step(bi, hs, g), n - 1), 0))

    cast_specs = [
        pl.BlockSpec((w_up.shape[0], up_cols), lambda bi, hs, g: (0, step(bi, hs, g))),
        pl.BlockSpec((down_rows, w_down.shape[1]), lambda bi, hs, g: (step(bi, hs, g), 0)),
    ] + [rows_spec(w) for w in row_sliced]
    cast_args = [w_up, w_down] + list(row_sliced)
    tok = pltpu.VMEM((seq, HEAD_DIM), F32)
    win = pltpu.VMEM((3, ATTN_QB, ATTN_KW), F32)
    return pl.pallas_call(
        functools.partial(_attn_body, seq, len(cast_args)),
        grid=(b, HEADS_PER_GROUP, N_GROUPS),
        in_specs=[head_spec(Q_COL), head_spec(K_COL), head_spec(V_COL),
                  pl.BlockSpec((N_GROUPS, ATTN_PERM_ROWS, ATTN_PERM_ROWS),
                               lambda bi, hs, g: (0, 0, 0))] + cast_specs,
        out_specs=[pl.BlockSpec((None, seq, HEAD_DIM), lambda bi, hs, g: (hs, bi, 0))] + cast_specs,
        out_shape=[jax.ShapeDtypeStruct((HEADS_PER_GROUP, b * seq, HEAD_DIM), BF16)]
        + [jax.ShapeDtypeStruct(w.shape, BF16) for w in cast_args],
        scratch_shapes=[
            pltpu.VMEM((seq, HEAD_DIM), BF16),
            pltpu.VMEM((seq + 2 * RADIUS * max_d, HEAD_DIM), BF16),
            pltpu.VMEM((seq + 2 * RADIUS * max_d, 2 * HEAD_DIM), BF16),
            win, win,
            tok, tok, tok, tok, tok, tok,
        ],
        compiler_params=pltpu.CompilerParams(
            dimension_semantics=("arbitrary", "arbitrary", "arbitrary"),
            vmem_limit_bytes=VMEM_LIMIT_BYTES),
        name="attn",
    )(proj_h, proj_h, proj_h, _perm_matrices(), *cast_args)


def _mix_body(seq, attn_ref, p_ref, pprev_ref, pnext_ref, ga_ref, gb_ref, x_ref,
              wa_ref, pw_ref, ps_ref, wp_ref, wo_ref, gn_ref, h_ref, z_ref, pad, pm):
    i = pl.program_id(0)
    tiles_per_seq = seq // TM_MIX
    ti = i % tiles_per_seq
    for g in range(len(POOL_WINDOWS)):
        cs = slice(g * POOL_GROUP_WIDTH, (g + 1) * POOL_GROUP_WIDTH)
        pad[0:POOL_HALO, cs] = jnp.where(ti == 0, 0.0, pprev_ref[g].astype(F32))
        pad[POOL_HALO: POOL_HALO + TM_MIX, cs] = p_ref[g].astype(F32)
        pad[POOL_HALO + TM_MIX:, cs] = jnp.where(
            ti == tiles_per_seq - 1, 0.0, pnext_ref[g].astype(F32))
    attn = jnp.concatenate([attn_ref[hs] for hs in range(HEADS_PER_GROUP)], axis=1)

    ya_cols = D_MODEL // len(POOL_WINDOWS)
    ya_parts = []
    t = ti * TM_MIX + lax.broadcasted_iota(jnp.int32, (TM_MIX, POOL_GROUP_WIDTH), 0)
    for g, w in enumerate(POOL_WINDOWS):
        ya_parts.append(jnp.dot(attn, wa_ref[:, g * ya_cols: (g + 1) * ya_cols],
                                preferred_element_type=F32))
        cs = slice(g * POOL_GROUP_WIDTH, (g + 1) * POOL_GROUP_WIDTH)
        tot = pad[POOL_HALO - w // 2: POOL_HALO - w // 2 + TM_MIX, cs]
        for j in range(-w // 2 + 1, w // 2):
            tot = tot + pad[POOL_HALO + j: POOL_HALO + j + TM_MIX, cs]
        cnt = (jnp.minimum(t + w // 2, seq) - jnp.maximum(t - w // 2, 0)).astype(F32)
        pooled = tot / cnt - pad[POOL_HALO: POOL_HALO + TM_MIX, cs]
        pmg = jnp.dot(pooled.astype(BF16), pw_ref[g], preferred_element_type=F32) * ps_ref[:, cs]
        pm[:, cs] = pmg.astype(BF16)

    y_a = jnp.concatenate(ya_parts, axis=1)
    y_b = jnp.dot(pm[...], wp_ref[...], preferred_element_type=F32)
    rc = TM_MIX // MIX_ROW_CHUNKS
    for c in range(MIX_ROW_CHUNKS):
        rows = slice(c * rc, (c + 1) * rc)
        mixed = []
        for hd in range(D_MODEL // HEAD_DIM):
            sl = slice(hd * HEAD_DIM, (hd + 1) * HEAD_DIM)
            mixed.append((ga_ref[hd, rows, :].astype(F32) * y_a[rows, sl]
                          + gb_ref[hd, rows, :].astype(F32) * y_b[rows, sl]).astype(BF16))
        mixed = jnp.concatenate(mixed, axis=1)
        h = x_ref[rows, :] + jnp.dot(mixed, wo_ref[...], preferred_element_type=F32)
        h_ref[rows, :] = h
        z_ref[rows, :] = _rmsnorm(h, gn_ref[...]).astype(BF16)


def _mix(attn2, proj2, x2, wa, pw, ps, wp, wo, gn, seq):
    t = x2.shape[0]
    halo_blocks = TM_MIX // POOL_HALO
    n_halo = t // POOL_HALO
    p_blk = P_COL // POOL_WIDTH
    pool_planes = POOL_WIDTH // HEAD_DIM
    gate_planes = D_MODEL // HEAD_DIM
    const = dict(pipeline_mode=pl.Buffered(1))
    return pl.pallas_call(
        functools.partial(_mix_body, seq),
        grid=(t // TM_MIX,),
        in_specs=[
            pl.BlockSpec((HEADS_PER_GROUP, TM_MIX, HEAD_DIM), lambda i: (0, i, 0)),
            pl.BlockSpec((pool_planes, TM_MIX, HEAD_DIM), lambda i: (p_blk, i, 0)),
            pl.BlockSpec((pool_planes, POOL_HALO, HEAD_DIM),
                         lambda i: (p_blk, jnp.maximum(i * halo_blocks - 1, 0), 0)),
            pl.BlockSpec((pool_planes, POOL_HALO, HEAD_DIM),
                         lambda i: (p_blk, jnp.minimum((i + 1) * halo_blocks, n_halo - 1), 0)),
            pl.BlockSpec((gate_planes, TM_MIX, HEAD_DIM), lambda i: (0, i, 0)),
            pl.BlockSpec((gate_planes, TM_MIX, HEAD_DIM), lambda i: (1, i, 0)),
            pl.BlockSpec((TM_MIX, D_MODEL), lambda i: (i, 0)),
            pl.BlockSpec((ATTN_OUT_WIDTH, D_MODEL), lambda i: (0, 0), **const),
            pl.BlockSpec((len(POOL_WINDOWS), POOL_GROUP_WIDTH, POOL_GROUP_WIDTH),
                         lambda i: (0, 0, 0), **const),
            pl.BlockSpec((1, POOL_WIDTH), lambda i: (0, 0)),
            pl.BlockSpec((POOL_WIDTH, D_MODEL), lambda i: (0, 0), **const),
            pl.BlockSpec((D_MODEL, D_MODEL), lambda i: (0, 0), **const),
            pl.BlockSpec((1, D_MODEL), lambda i: (0, 0)),
        ],
        out_specs=[
            pl.BlockSpec((TM_MIX, D_MODEL), lambda i: (i, 0)),
            pl.BlockSpec((TM_MIX, D_MODEL), lambda i: (i, 0)),
        ],
        out_shape=[
            jax.ShapeDtypeStruct((t, D_MODEL), F32),
            jax.ShapeDtypeStruct((t, D_MODEL), BF16),
        ],
        scratch_shapes=[
            pltpu.VMEM((TM_MIX + 2 * POOL_HALO, POOL_WIDTH), F32),
            pltpu.VMEM((TM_MIX, POOL_WIDTH), BF16),
        ],
        compiler_params=pltpu.CompilerParams(
            dimension_semantics=("parallel",), vmem_limit_bytes=VMEM_LIMIT_BYTES),
        name="mix",
    )(attn2, proj2, proj2, proj2, proj2, proj2, x2, wa, pw, ps, wp, wo, gn)


def _ffn_body(seq, z_ref, zprev_ref, znext_ref, h_ref, wa_ref, wb_ref, cw_ref, cb_ref, wd_ref,
              gn_ref, o_ref, zext, aext):
    i = pl.program_id(0)
    f = pl.program_id(1)
    tiles_per_seq = seq // TM_FFN
    ti = i % tiles_per_seq

    @pl.when(f == 0)
    def _():
        zero_rows = jnp.zeros((CONV_HALO, D_MODEL), BF16)
        zext[0:CONV_HALO, :] = jnp.where(ti == 0, zero_rows, zprev_ref[...])
        zext[CONV_HALO: CONV_HALO + TM_FFN, :] = z_ref[...]
        zext[CONV_HALO + TM_FFN:, :] = jnp.where(ti == tiles_per_seq - 1, zero_rows, znext_ref[...])
        o_ref[...] = h_ref[...]

    aext[...] = jnp.dot(zext[...], wa_ref[...], preferred_element_type=F32)
    b = jnp.dot(zext[CONV_HALO: CONV_HALO + TM_FFN, :], wb_ref[...], preferred_element_type=F32)
    cw = cw_ref[...]
    a = (aext[CONV_HALO - 1: CONV_HALO - 1 + TM_FFN, :] * cw[0:1, :]
         + aext[CONV_HALO: CONV_HALO + TM_FFN, :] * cw[1:2, :]
         + aext[CONV_HALO + 1: CONV_HALO + 1 + TM_FFN, :] * cw[2:3, :]
         + cb_ref[...])
    gelu = 0.5 * a * (1.0 + lax.erf(a * np.float32(np.sqrt(0.5))))
    o_ref[...] += jnp.dot((gelu * b).astype(BF16), wd_ref[...], preferred_element_type=F32)

    @pl.when(f == pl.num_programs(1) - 1)
    def _():
        o_ref[...] = _rmsnorm(o_ref[...], gn_ref[...])


def _ffn(z2, h2, w_up, cw, cb, w_down, gn, seq):
    t = z2.shape[0]
    halo_blocks = TM_FFN // CONV_HALO
    n_halo = t // CONV_HALO
    nf = D_FF // TF_FFN
    return pl.pallas_call(
        functools.partial(_ffn_body, seq),
        grid=(t // TM_FFN, nf),
        in_specs=[
            pl.BlockSpec((TM_FFN, D_MODEL), lambda i, f: (i, 0)),
            pl.BlockSpec((CONV_HALO, D_MODEL), lambda i, f: (jnp.maximum(i * halo_blocks - 1, 0), 0)),
            pl.BlockSpec((CONV_HALO, D_MODEL),
                         lambda i, f: (jnp.minimum((i + 1) * halo_blocks, n_halo - 1), 0)),
            pl.BlockSpec((TM_FFN, D_MODEL), lambda i, f: (i, 0)),
            pl.BlockSpec((D_MODEL, TF_FFN), lambda i, f: (0, f)),
            pl.BlockSpec((D_MODEL, TF_FFN), lambda i, f: (0, nf + f)),
            pl.BlockSpec((3, TF_FFN), lambda i, f: (0, f)),
            pl.BlockSpec((1, TF_FFN), lambda i, f: (0, f)),
            pl.BlockSpec((TF_FFN, D_MODEL), lambda i, f: (f, 0)),
            pl.BlockSpec((1, D_MODEL), lambda i, f: (0, 0)),
        ],
        out_specs=pl.BlockSpec((TM_FFN, D_MODEL), lambda i, f: (i, 0)),
        out_shape=jax.ShapeDtypeStruct((t, D_MODEL), F32),
        scratch_shapes=[
            pltpu.VMEM((TM_FFN + 2 * CONV_HALO, D_MODEL), BF16),
            pltpu.VMEM((TM_FFN + 2 * CONV_HALO, TF_FFN), F32),
        ],
        compiler_params=pltpu.CompilerParams(
            dimension_semantics=("parallel", "arbitrary"), vmem_limit_bytes=VMEM_LIMIT_BYTES),
        name="ffn",
    )(z2, z2, z2, h2, w_up, w_up, cw, cb, w_down, gn)


def _rotary_tables(seq):
    inv_freq = ROPE_THETA ** (-np.arange(0, ROT_DIM, 2, dtype=np.float64) / ROT_DIM)
    ang = np.arange(seq, dtype=np.float64)[:, None] * inv_freq[None, :]
    cos, sin = np.cos(ang), np.sin(ang)
    ones = np.ones((seq, HEAD_DIM - ROT_DIM))
    zeros_h = np.zeros((seq, ROT_HALF))
    zeros_r = np.zeros((seq, HEAD_DIM - ROT_DIM))
    cos_t = np.concatenate([cos, cos, ones], axis=-1)
    sina_t = np.concatenate([-sin, zeros_h, zeros_r], axis=-1)
    sinb_t = np.concatenate([zeros_h, sin, zeros_r], axis=-1)
    return tuple(jnp.asarray(t, dtype=F32) for t in (cos_t, sina_t, sinb_t))


def kernel(x, norm_mix_g, w_in, w_attn_out, pool_w, pool_scale, w_pool_out, w_gate, w_out,
           norm_ffn_g, w_up, conv_w, conv_b, w_down, norm_final_g):
    b, seq, d = x.shape
    assert d == D_MODEL and seq % TM_IN == 0 and seq % TM_FFN == 0 and seq % TM_MIX == 0
    assert all(seq % (dil * ATTN_QB) == 0 for _, dil in DILATION_PAIRS)
    t = b * seq
    x2 = x.reshape(t, d)

    cos_t, sina_t, sinb_t = _rotary_tables(seq)
    proj = _inproj(x2, norm_mix_g.reshape(1, d), w_gate.astype(BF16), w_in.astype(BF16),
                   cos_t, sina_t, sinb_t, seq)
    attn, w_up_b, w_down_b, w_attn_out_b, w_pool_out_b, w_out_b = _attention(
        proj, b, seq, w_up, w_down, (w_attn_out, w_pool_out, w_out))
    h, z = _mix(attn, proj, x2,
                w_attn_out_b, pool_w.astype(BF16), pool_scale.reshape(1, POOL_WIDTH),
                w_pool_out_b, w_out_b, norm_ffn_g.reshape(1, d), seq)
    out = _ffn(z, h, w_up_b, conv_w, conv_b.reshape(1, D_FF), w_down_b,
               norm_final_g.reshape(1, d), seq)
    return out.reshape(b, seq, d)
```

```python
import functools

import numpy as np
import jax
import jax.numpy as jnp
from jax import lax
from jax.experimental import pallas as pl
from jax.experimental.pallas import tpu as pltpu

F32 = jnp.float32
BF16 = jnp.bfloat16

D_MODEL = 2048
HEAD_DIM = 128
DILATION_PAIRS = ((128, 1), (512, 4), (2048, 16))
N_GROUPS = len(DILATION_PAIRS)
HEADS_PER_GROUP = 4
N_ATTN_HEADS = N_GROUPS * HEADS_PER_GROUP
ATTN_WIDTH = N_ATTN_HEADS * HEAD_DIM
ATTN_OUT_WIDTH = HEADS_PER_GROUP * HEAD_DIM
ROT_DIM = HEAD_DIM // 4
ROT_HALF = ROT_DIM // 2
ROPE_THETA = 500000.0
POOL_WINDOWS = (2, 4, 8, 16)
POOL_WIDTH = 512
POOL_GROUP_WIDTH = 128
D_FF = 3 * D_MODEL
NORM_EPS = 1e-6
NEG_INF = -1e30
RADIUS = 64
assert all(w // 2 // d == RADIUS for w, d in DILATION_PAIRS)

LANES = 128
BF16_SUBLANES = 16
MXU_COLS = 256
VMEM_LIMIT_BYTES = 56 * 1024 * 1024

GATE_COLS = 2 * D_MODEL
Q_COL = GATE_COLS
K_COL = Q_COL + ATTN_WIDTH
V_COL = K_COL + ATTN_WIDTH
P_COL = V_COL + ATTN_WIDTH
PROJ_COLS = P_COL + POOL_WIDTH

TM_IN, TN_IN = 1024, 1024
IN_ROW_CHUNKS = 4
ROT_TILE_LO = GATE_COLS // TN_IN
ROT_TILE_HI = V_COL // TN_IN
assert GATE_COLS % TN_IN == 0 and V_COL % TN_IN == 0 and PROJ_COLS % TN_IN == 0
ATTN_QB = 128
ATTN_KW = ATTN_QB + 2 * RADIUS
ATTN_PERM_ROWS = 256
ATTN_MERGE_ROWS = 256
CAST_ROWS = 128
TM_MIX = 512
MIX_ROW_CHUNKS = 2
POOL_HALO = BF16_SUBLANES
TM_FFN, TF_FFN = 512, 1024
CONV_HALO = BF16_SUBLANES


def _rmsnorm(x, g):
    return x * lax.rsqrt(jnp.mean(x * x, axis=-1, keepdims=True) + NORM_EPS) * g


def _inproj_body(x_ref, g_ref, wg_ref, wi_ref, cos_ref, sina_ref, sinb_ref, o_ref, u_ref):
    j = pl.program_id(1)

    def chunked(w_ref, epilogue):
        for c in range(IN_ROW_CHUNKS):
            rows = slice(c * (TM_IN // IN_ROW_CHUNKS), (c + 1) * (TM_IN // IN_ROW_CHUNKS))
            acc = jnp.dot(u_ref[rows, :], w_ref[...], preferred_element_type=F32)
            epilogue(acc, rows)

    def sigmoid(acc, rows):
        o_ref[rows, :] = jax.nn.sigmoid(acc).astype(BF16)

    def rotary(acc, rows):
        c = cos_ref[rows, :]
        sa = sina_ref[rows, :]
        sb = sinb_ref[rows, :]
        for hd in range(TN_IN // HEAD_DIM):
            sl = slice(hd * HEAD_DIM, (hd + 1) * HEAD_DIM)
            blk = acc[:, sl]
            up = pltpu.roll(blk, HEAD_DIM - ROT_HALF, axis=1)
            dn = pltpu.roll(blk, ROT_HALF, axis=1)
            o_ref[rows, sl] = (blk * c + up * sa + dn * sb).astype(BF16)

    def plain(acc, rows):
        o_ref[rows, :] = acc.astype(BF16)

    @pl.when(j == 0)
    def _():
        for c in range(IN_ROW_CHUNKS):
            rows = slice(c * (TM_IN // IN_ROW_CHUNKS), (c + 1) * (TM_IN // IN_ROW_CHUNKS))
            u = _rmsnorm(x_ref[rows, :], g_ref[...]).astype(BF16)
            u_ref[rows, :] = u
            sigmoid(jnp.dot(u, wg_ref[...], preferred_element_type=F32), rows)

    @pl.when((j > 0) & (j < ROT_TILE_LO))
    def _():
        chunked(wg_ref, sigmoid)

    @pl.when((j >= ROT_TILE_LO) & (j < ROT_TILE_HI))
    def _():
        chunked(wi_ref, rotary)

    @pl.when(j >= ROT_TILE_HI)
    def _():
        chunked(wi_ref, plain)


def _inproj(x2, g, w_gate, w_in, cos_t, sina_t, sinb_t, seq):
    t = x2.shape[0]
    pos_tiles = seq // TM_IN
    gate_tiles = GATE_COLS // TN_IN
    table_spec = pl.BlockSpec((TM_IN, HEAD_DIM), lambda i, j: (i % pos_tiles, 0))
    return pl.pallas_call(
        _inproj_body,
        grid=(t // TM_IN, PROJ_COLS // TN_IN),
        in_specs=[
            pl.BlockSpec((TM_IN, D_MODEL), lambda i, j: (i, 0)),
            pl.BlockSpec((1, D_MODEL), lambda i, j: (0, 0)),
            pl.BlockSpec((D_MODEL, TN_IN), lambda i, j: (0, jnp.minimum(j, gate_tiles - 1))),
            pl.BlockSpec((D_MODEL, TN_IN), lambda i, j: (0, jnp.maximum(j - gate_tiles, 0))),
            table_spec, table_spec, table_spec,
        ],
        out_specs=pl.BlockSpec((TM_IN, TN_IN), lambda i, j: (i, j)),
        out_shape=jax.ShapeDtypeStruct((t, PROJ_COLS), BF16),
        scratch_shapes=[pltpu.VMEM((TM_IN, D_MODEL), BF16)],
        compiler_params=pltpu.CompilerParams(
            dimension_semantics=("parallel", "arbitrary"), vmem_limit_bytes=VMEM_LIMIT_BYTES),
        name="inproj",
    )(x2, g, w_gate, w_in, cos_t, sina_t, sinb_t)


def _perm_matrices():
    mats = []
    for _, d in DILATION_PAIRS:
        seg = ATTN_PERM_ROWS // d
        p = np.zeros((ATTN_PERM_ROWS, ATTN_PERM_ROWS), np.float32)
        for r in range(d):
            for l in range(seg):
                p[r * seg + l, l * d + r] = 1.0
        mats.append(p)
    return jnp.asarray(np.stack(mats), dtype=BF16)


def _attn_group(gi, d, seq, q_ref, k_ref, v_ref, perm_ref, qd, kd, vd, mask, bias, og, lg):
    L = seq // d
    LP = L + 2 * RADIUS
    nqb = L // ATTN_QB
    seg = ATTN_PERM_ROWS // d

    def deinterleave(src_ref, dst_ref, padded):
        for c in range(seq // ATTN_PERM_ROWS):
            x = src_ref[c * ATTN_PERM_ROWS: (c + 1) * ATTN_PERM_ROWS, :]
            y = jnp.dot(perm_ref[gi], x, preferred_element_type=F32).astype(BF16)
            for r in range(d):
                base = (r * LP + RADIUS if padded else r * L) + c * seg
                dst_ref[base: base + seg, 0:HEAD_DIM] = y[r * seg: (r + 1) * seg, :]

    zero_pad = jnp.zeros((RADIUS, HEAD_DIM), BF16)
    for r in range(d):
        for dst in (kd, vd):
            dst[r * LP: r * LP + RADIUS, 0:HEAD_DIM] = zero_pad
            dst[r * LP + RADIUS + L: (r + 1) * LP, 0:HEAD_DIM] = zero_pad
    vd[0: d * LP, HEAD_DIM:] = jnp.ones((d * LP, HEAD_DIM), BF16)
    if d == 1:
        q_src = q_ref
        kd[RADIUS: RADIUS + seq, :] = k_ref[...]
        vd[RADIUS: RADIUS + seq, 0:HEAD_DIM] = v_ref[...]
    else:
        q_src = qd
        deinterleave(q_ref, qd, False)
        deinterleave(k_ref, kd, True)
        deinterleave(v_ref, vd, True)

    scale = HEAD_DIM ** -0.5

    for blk in range(seq // ATTN_QB):
        r, qb = divmod(blk, nqb)
        q0 = r * L + qb * ATTN_QB
        k0 = r * LP + qb * ATTN_QB
        q = q_src[q0: q0 + ATTN_QB, :]
        kw = kd[k0: k0 + ATTN_KW, :]
        vw = vd[k0: k0 + ATTN_KW, :]
        kind = 1 if qb == 0 else (2 if qb == nqb - 1 else 0)
        s = lax.dot_general(q, kw, (((1,), (1,)), ((), ())), preferred_element_type=F32)
        s = s * scale + bias[kind]
        m = jnp.max(s, axis=-1, keepdims=True)
        p = jnp.exp(s - m) * mask[kind]
        pv = jnp.dot(p.astype(BF16), vw, preferred_element_type=F32)
        den = pv[:, HEAD_DIM:]
        o = pv[:, :HEAD_DIM] / den
        lse = m + jnp.log(den)
        if d == 1:
            og[gi][q0: q0 + ATTN_QB, :] = o
            lg[gi][q0: q0 + ATTN_QB, :] = lse
        else:
            t0 = qb * (ATTN_QB * d) + r
            og[gi][pl.ds(t0, ATTN_QB, stride=d), :] = o
            lg[gi][pl.ds(t0, ATTN_QB, stride=d), :] = lse


def _attn_body(seq, n_cast, q_ref, k_ref, v_ref, perm_ref, *refs):
    cast_in, o_ref, cast_out = refs[:n_cast], refs[n_cast], refs[n_cast + 1: 2 * n_cast + 1]
    qd, kd, vd, mask, bias, og0, og1, og2, lg0, lg1, lg2 = refs[2 * n_cast + 1:]
    g = pl.program_id(2)
    og = (og0, og1, og2)
    lg = (lg0, lg1, lg2)

    @pl.when(g > 0)
    def _():
        for src, dst in zip(cast_in, cast_out):
            dst[...] = src[...].astype(BF16)

    row = lax.broadcasted_iota(jnp.int32, (ATTN_QB, ATTN_KW), 0)
    col = lax.broadcasted_iota(jnp.int32, (ATTN_QB, ATTN_KW), 1)
    band = (col >= row) & (col <= row + 2 * RADIUS)
    for kind, valid in enumerate((band, band & (col >= RADIUS), band & (col < ATTN_QB + RADIUS))):
        mask[kind] = jnp.where(valid, 1.0, 0.0)
        bias[kind] = jnp.where(valid, 0.0, NEG_INF)

    for gi, (_, d) in enumerate(DILATION_PAIRS):
        @pl.when(g == gi)
        def _(gi=gi, d=d):
            _attn_group(gi, d, seq, q_ref, k_ref, v_ref, perm_ref, qd, kd, vd, mask, bias, og, lg)

    @pl.when(g == N_GROUPS - 1)
    def _():
        for c in range(seq // ATTN_MERGE_ROWS):
            sl = slice(c * ATTN_MERGE_ROWS, (c + 1) * ATTN_MERGE_ROWS)
            l0, l1, l2 = lg0[sl, :], lg1[sl, :], lg2[sl, :]
            mx = jnp.maximum(jnp.maximum(l0, l1), l2)
            e0, e1, e2 = jnp.exp(l0 - mx), jnp.exp(l1 - mx), jnp.exp(l2 - mx)
            num = e0 * og0[sl, :] + e1 * og1[sl, :] + e2 * og2[sl, :]
            o_ref[sl, :] = (num / (e0 + e1 + e2)).astype(BF16)


def _attention(proj3, w_up, w_down, row_sliced):
    b, seq, _ = proj3.shape
    max_d = max(d for _, d in DILATION_PAIRS)
    steps = b * HEADS_PER_GROUP * (N_GROUPS - 1)
    up_cols = w_up.shape[1] // steps
    down_rows = w_down.shape[0] // steps
    assert w_up.shape[1] % (steps * LANES) == 0 and w_down.shape[0] % (steps * BF16_SUBLANES) == 0

    def head_spec(col0):
        blk0 = col0 // HEAD_DIM
        return pl.BlockSpec((None, seq, HEAD_DIM),
                            lambda bi, hs, g: (bi, 0, blk0 + g * HEADS_PER_GROUP + hs))

    def step(bi, hs, g):
        return (bi * HEADS_PER_GROUP + hs) * (N_GROUPS - 1) + jnp.maximum(g - 1, 0)

    def rows_spec(w):
        n = w.shape[0] // CAST_ROWS
        assert w.shape[0] % CAST_ROWS == 0 and n <= steps
        return pl.BlockSpec((CAST_ROWS, w.shape[1]),
                            lambda bi, hs, g: (jnp.minimum(step(bi, hs, g), n - 1), 0))

    cast_specs = [
        pl.BlockSpec((w_up.shape[0], up_cols), lambda bi, hs, g: (0, step(bi, hs, g))),
        pl.BlockSpec((down_rows, w_down.shape[1]), lambda bi, hs, g: (step(bi, hs, g), 0)),
    ] + [rows_spec(w) for w in row_sliced]
    cast_args = [w_up, w_down] + list(row_sliced)
    tok = pltpu.VMEM((seq, HEAD_DIM), F32)
    win = pltpu.VMEM((3, ATTN_QB, ATTN_KW), F32)
    return pl.pallas_call(
        functools.partial(_attn_body, seq, len(cast_args)),
        grid=(b, HEADS_PER_GROUP, N_GROUPS),
        in_specs=[head_spec(Q_COL), head_spec(K_COL), head_spec(V_COL),
                  pl.BlockSpec((N_GROUPS, ATTN_PERM_ROWS, ATTN_PERM_ROWS),
                               lambda bi, hs, g: (0, 0, 0))] + cast_specs,
        out_specs=[pl.BlockSpec((None, seq, HEAD_DIM), lambda bi, hs, g: (bi, 0, hs))] + cast_specs,
        out_shape=[jax.ShapeDtypeStruct((b, seq, ATTN_OUT_WIDTH), BF16)]
        + [jax.ShapeDtypeStruct(w.shape, BF16) for w in cast_args],
        scratch_shapes=[
            pltpu.VMEM((seq, HEAD_DIM), BF16),
            pltpu.VMEM((seq + 2 * RADIUS * max_d, HEAD_DIM), BF16),
            pltpu.VMEM((seq + 2 * RADIUS * max_d, 2 * HEAD_DIM), BF16),
            win, win,
            tok, tok, tok, tok, tok, tok,
        ],
        compiler_params=pltpu.CompilerParams(
            dimension_semantics=("arbitrary", "arbitrary", "arbitrary"),
            vmem_limit_bytes=VMEM_LIMIT_BYTES),
        name="attn",
    )(proj3, proj3, proj3, _perm_matrices(), *cast_args)


def _mix_body(seq, attn_ref, p_ref, pprev_ref, pnext_ref, ga_ref, gb_ref, x_ref,
              wa_ref, pw_ref, ps_ref, wp_ref, wo_ref, gn_ref, h_ref, z_ref, pad, pm):
    i = pl.program_id(0)
    tiles_per_seq = seq // TM_MIX
    ti = i % tiles_per_seq
    pad[0:POOL_HALO, :] = jnp.where(ti == 0, 0.0, pprev_ref[...].astype(F32))
    pad[POOL_HALO: POOL_HALO + TM_MIX, :] = p_ref[...].astype(F32)
    pad[POOL_HALO + TM_MIX:, :] = jnp.where(ti == tiles_per_seq - 1, 0.0, pnext_ref[...].astype(F32))

    ya_cols = D_MODEL // len(POOL_WINDOWS)
    ya_parts = []
    t = ti * TM_MIX + lax.broadcasted_iota(jnp.int32, (TM_MIX, POOL_GROUP_WIDTH), 0)
    for g, w in enumerate(POOL_WINDOWS):
        ya_parts.append(jnp.dot(attn_ref[...], wa_ref[:, g * ya_cols: (g + 1) * ya_cols],
                                preferred_element_type=F32))
        cs = slice(g * POOL_GROUP_WIDTH, (g + 1) * POOL_GROUP_WIDTH)
        tot = pad[POOL_HALO - w // 2: POOL_HALO - w // 2 + TM_MIX, cs]
        for j in range(-w // 2 + 1, w // 2):
            tot = tot + pad[POOL_HALO + j: POOL_HALO + j + TM_MIX, cs]
        cnt = (jnp.minimum(t + w // 2, seq) - jnp.maximum(t - w // 2, 0)).astype(F32)
        pooled = tot / cnt - pad[POOL_HALO: POOL_HALO + TM_MIX, cs]
        pmg = jnp.dot(pooled.astype(BF16), pw_ref[g], preferred_element_type=F32) * ps_ref[:, cs]
        pm[:, cs] = pmg.astype(BF16)

    y_a = jnp.concatenate(ya_parts, axis=1)
    y_b = jnp.dot(pm[...], wp_ref[...], preferred_element_type=F32)
    rc = TM_MIX // MIX_ROW_CHUNKS
    for c in range(MIX_ROW_CHUNKS):
        rows = slice(c * rc, (c + 1) * rc)
        mixed = ga_ref[rows, :].astype(F32) * y_a[rows] + gb_ref[rows, :].astype(F32) * y_b[rows]
        h = x_ref[rows, :] + jnp.dot(mixed.astype(BF16), wo_ref[...], preferred_element_type=F32)
        h_ref[rows, :] = h
        z_ref[rows, :] = _rmsnorm(h, gn_ref[...]).astype(BF16)


def _mix(attn2, proj2, x2, wa, pw, ps, wp, wo, gn, seq):
    t = x2.shape[0]
    halo_blocks = TM_MIX // POOL_HALO
    n_halo = t // POOL_HALO
    p_blk = P_COL // POOL_WIDTH
    const = dict(pipeline_mode=pl.Buffered(1))
    return pl.pallas_call(
        functools.partial(_mix_body, seq),
        grid=(t // TM_MIX,),
        in_specs=[
            pl.BlockSpec((TM_MIX, ATTN_OUT_WIDTH), lambda i: (i, 0)),
            pl.BlockSpec((TM_MIX, POOL_WIDTH), lambda i: (i, p_blk)),
            pl.BlockSpec((POOL_HALO, POOL_WIDTH),
                         lambda i: (jnp.maximum(i * halo_blocks - 1, 0), p_blk)),
            pl.BlockSpec((POOL_HALO, POOL_WIDTH),
                         lambda i: (jnp.minimum((i + 1) * halo_blocks, n_halo - 1), p_blk)),
            pl.BlockSpec((TM_MIX, D_MODEL), lambda i: (i, 0)),
            pl.BlockSpec((TM_MIX, D_MODEL), lambda i: (i, 1)),
            pl.BlockSpec((TM_MIX, D_MODEL), lambda i: (i, 0)),
            pl.BlockSpec((ATTN_OUT_WIDTH, D_MODEL), lambda i: (0, 0), **const),
            pl.BlockSpec((len(POOL_WINDOWS), POOL_GROUP_WIDTH, POOL_GROUP_WIDTH),
                         lambda i: (0, 0, 0), **const),
            pl.BlockSpec((1, POOL_WIDTH), lambda i: (0, 0)),
            pl.BlockSpec((POOL_WIDTH, D_MODEL), lambda i: (0, 0), **const),
            pl.BlockSpec((D_MODEL, D_MODEL), lambda i: (0, 0), **const),
            pl.BlockSpec((1, D_MODEL), lambda i: (0, 0)),
        ],
        out_specs=[
            pl.BlockSpec((TM_MIX, D_MODEL), lambda i: (i, 0)),
            pl.BlockSpec((TM_MIX, D_MODEL), lambda i: (i, 0)),
        ],
        out_shape=[
            jax.ShapeDtypeStruct((t, D_MODEL), F32),
            jax.ShapeDtypeStruct((t, D_MODEL), BF16),
        ],
        scratch_shapes=[
            pltpu.VMEM((TM_MIX + 2 * POOL_HALO, POOL_WIDTH), F32),
            pltpu.VMEM((TM_MIX, POOL_WIDTH), BF16),
        ],
        compiler_params=pltpu.CompilerParams(
            dimension_semantics=("parallel",), vmem_limit_bytes=VMEM_LIMIT_BYTES),
        name="mix",
    )(attn2, proj2, proj2, proj2, proj2, proj2, x2, wa, pw, ps, wp, wo, gn)


def _ffn_body(seq, z_ref, zprev_ref, znext_ref, h_ref, wa_ref, wb_ref, cw_ref, cb_ref, wd_ref,
              gn_ref, o_ref, zext, aext):
    i = pl.program_id(0)
    f = pl.program_id(1)
    tiles_per_seq = seq // TM_FFN
    ti = i % tiles_per_seq

    @pl.when(f == 0)
    def _():
        zero_rows = jnp.zeros((CONV_HALO, D_MODEL), BF16)
        zext[0:CONV_HALO, :] = jnp.where(ti == 0, zero_rows, zprev_ref[...])
        zext[CONV_HALO: CONV_HALO + TM_FFN, :] = z_ref[...]
        zext[CONV_HALO + TM_FFN:, :] = jnp.where(ti == tiles_per_seq - 1, zero_rows, znext_ref[...])
        o_ref[...] = h_ref[...]

    aext[...] = jnp.dot(zext[...], wa_ref[...], preferred_element_type=F32)
    b = jnp.dot(zext[CONV_HALO: CONV_HALO + TM_FFN, :], wb_ref[...], preferred_element_type=F32)
    cw = cw_ref[...]
    a = (aext[CONV_HALO - 1: CONV_HALO - 1 + TM_FFN, :] * cw[0:1, :]
         + aext[CONV_HALO: CONV_HALO + TM_FFN, :] * cw[1:2, :]
         + aext[CONV_HALO + 1: CONV_HALO + 1 + TM_FFN, :] * cw[2:3, :]
         + cb_ref[...])
    gelu = 0.5 * a * (1.0 + lax.erf(a * np.float32(np.sqrt(0.5))))
    o_ref[...] += jnp.dot((gelu * b).astype(BF16), wd_ref[...], preferred_element_type=F32)

    @pl.when(f == pl.num_programs(1) - 1)
    def _():
        o_ref[...] = _rmsnorm(o_ref[...], gn_ref[...])


def _ffn(z2, h2, w_up, cw, cb, w_down, gn, seq):
    t = z2.shape[0]
    halo_blocks = TM_FFN // CONV_HALO
    n_halo = t // CONV_HALO
    nf = D_FF // TF_FFN
    return pl.pallas_call(
        functools.partial(_ffn_body, seq),
        grid=(t // TM_FFN, nf),
        in_specs=[
            pl.BlockSpec((TM_FFN, D_MODEL), lambda i, f: (i, 0)),
            pl.BlockSpec((CONV_HALO, D_MODEL), lambda i, f: (jnp.maximum(i * halo_blocks - 1, 0), 0)),
            pl.BlockSpec((CONV_HALO, D_MODEL),
                         lambda i, f: (jnp.minimum((i + 1) * halo_blocks, n_halo - 1), 0)),
            pl.BlockSpec((TM_FFN, D_MODEL), lambda i, f: (i, 0)),
            pl.BlockSpec((D_MODEL, TF_FFN), lambda i, f: (0, f)),
            pl.BlockSpec((D_MODEL, TF_FFN), lambda i, f: (0, nf + f)),
            pl.BlockSpec((3, TF_FFN), lambda i, f: (0, f)),
            pl.BlockSpec((1, TF_FFN), lambda i, f: (0, f)),
            pl.BlockSpec((TF_FFN, D_MODEL), lambda i, f: (f, 0)),
            pl.BlockSpec((1, D_MODEL), lambda i, f: (0, 0)),
        ],
        out_specs=pl.BlockSpec((TM_FFN, D_MODEL), lambda i, f: (i, 0)),
        out_shape=jax.ShapeDtypeStruct((t, D_MODEL), F32),
        scratch_shapes=[
            pltpu.VMEM((TM_FFN + 2 * CONV_HALO, D_MODEL), BF16),
            pltpu.VMEM((TM_FFN + 2 * CONV_HALO, TF_FFN), F32),
        ],
        compiler_params=pltpu.CompilerParams(
            dimension_semantics=("parallel", "arbitrary"), vmem_limit_bytes=VMEM_LIMIT_BYTES),
        name="ffn",
    )(z2, z2, z2, h2, w_up, w_up, cw, cb, w_down, gn)


def _rotary_tables(seq):
    inv_freq = ROPE_THETA ** (-np.arange(0, ROT_DIM, 2, dtype=np.float64) / ROT_DIM)
    ang = np.arange(seq, dtype=np.float64)[:, None] * inv_freq[None, :]
    cos, sin = np.cos(ang), np.sin(ang)
    ones = np.ones((seq, HEAD_DIM - ROT_DIM))
    zeros_h = np.zeros((seq, ROT_HALF))
    zeros_r = np.zeros((seq, HEAD_DIM - ROT_DIM))
    cos_t = np.concatenate([cos, cos, ones], axis=-1)
    sina_t = np.concatenate([-sin, zeros_h, zeros_r], axis=-1)
    sinb_t = np.concatenate([zeros_h, sin, zeros_r], axis=-1)
    return tuple(jnp.asarray(t, dtype=F32) for t in (cos_t, sina_t, sinb_t))


def kernel(x, norm_mix_g, w_in, w_attn_out, pool_w, pool_scale, w_pool_out, w_gate, w_out,
           norm_ffn_g, w_up, conv_w, conv_b, w_down, norm_final_g):
    b, seq, d = x.shape
    assert d == D_MODEL and seq % TM_IN == 0 and seq % TM_FFN == 0 and seq % TM_MIX == 0
    assert all(seq % (dil * ATTN_QB) == 0 for _, dil in DILATION_PAIRS)
    t = b * seq
    x2 = x.reshape(t, d)

    cos_t, sina_t, sinb_t = _rotary_tables(seq)
    proj = _inproj(x2, norm_mix_g.reshape(1, d), w_gate.astype(BF16), w_in.astype(BF16),
                   cos_t, sina_t, sinb_t, seq)
    attn, w_up_b, w_down_b, w_attn_out_b, w_pool_out_b, w_out_b = _attention(
        proj.reshape(b, seq, PROJ_COLS), w_up, w_down, (w_attn_out, w_pool_out, w_out))
    h, z = _mix(attn.reshape(t, ATTN_OUT_WIDTH), proj, x2,
                w_attn_out_b, pool_w.astype(BF16), pool_scale.reshape(1, POOL_WIDTH),
                w_pool_out_b, w_out_b, norm_ffn_g.reshape(1, d), seq)
    out = _ffn(z, h, w_up_b, conv_w, conv_b.reshape(1, D_FF), w_down_b,
               norm_final_g.reshape(1, d), seq)
    return out.reshape(b, seq, d)
```
